```python
import math
import jax, jax.numpy as jnp
from jax import lax
import numpy as np

D_MODEL = 1024
BATCH = 16
SEQ = 2048
DEPTH = 2

GRID_W = 64
CTX_LEN = 256
N_MIXERS = 2
N_SSD_LAYERS = (DEPTH + N_MIXERS - 1) // N_MIXERS
N_ATTN_LAYERS = DEPTH // N_MIXERS
EPS = 1e-6
SSD_EXPAND = 2
D_INNER = SSD_EXPAND * D_MODEL
SSD_HEADDIM = 64
SSD_HEADS = D_INNER // SSD_HEADDIM
SSD_GROUPS = 8
SSD_HPG = SSD_HEADS // SSD_GROUPS
D_STATE = 128
CONV_W = 5
SSD_CHUNK = 128
CONV_DIM = D_INNER + 2 * SSD_GROUPS * D_STATE
SSD_IN_DIM = 2 * D_INNER + 2 * SSD_GROUPS * D_STATE + 2 * SSD_HEADS
HEAD_DIM = 64
N_Q_HEADS = D_MODEL // HEAD_DIM
N_KV_HEADS = 4
Q_PER_KV = N_Q_HEADS // N_KV_HEADS
D_ATTN = N_Q_HEADS * HEAD_DIM
QKV_DIM = (N_Q_HEADS + 2 * N_KV_HEADS) * HEAD_DIM
ROPE_AXIS_DIM = HEAD_DIM // 2
ROPE_THETA = 10000.0
Q_BLOCK = 128
N_EXPERTS = 16
D_EXPERT = D_MODEL
EC_CAPACITY = 2

kernel_name = "hybrid_ssd_gqa_ec_moe_dit"


def rmsnorm(x, w):
    xf = x.astype(jnp.float32)
    y = xf * lax.rsqrt(jnp.mean(xf * xf, axis=-1, keepdims=True) + EPS)
    return (y * w.astype(jnp.float32)).astype(x.dtype)


def dwconv_centred(u, w, b):
    out = lax.conv_general_dilated(
        u, w[:, None, :].astype(u.dtype), window_strides=(1,),
        padding=[(CONV_W // 2, CONV_W // 2)],
        dimension_numbers=('NWC', 'WIO', 'NWC'), feature_group_count=u.shape[-1])
    return out + b.astype(u.dtype)


def ssd_chunked(xs, dt, A, Bm, Cm, h0):
    b, L = xs.shape[:2]
    Q = SSD_CHUNK
    nc = L // Q
    x = xs.reshape(b, nc, Q, SSD_GROUPS, SSD_HPG, SSD_HEADDIM)
    dtc = dt.reshape(b, nc, Q, SSD_GROUPS, SSD_HPG)
    Bc = Bm.reshape(b, nc, Q, SSD_GROUPS, D_STATE)
    Cc = Cm.reshape(b, nc, Q, SSD_GROUPS, D_STATE)
    a = jnp.moveaxis(dtc * A.reshape(SSD_GROUPS, SSD_HPG), 2, -1)
    acum = jnp.cumsum(a, axis=-1)
    tri = jnp.tril(jnp.ones((Q, Q), dtype=bool))
    seg = jnp.exp(jnp.where(tri, acum[..., :, None] - acum[..., None, :], -jnp.inf))
    xdt = x * dtc[..., None].astype(x.dtype)
    cb = jnp.einsum('bclgn,bcsgn->bcgls', Cc, Bc)
    y_diag = jnp.einsum('bcgrls,bcsgrp->bclgrp', cb[:, :, :, None] * seg, xdt)
    decay_end = jnp.exp(acum[..., -1:] - acum)
    states = jnp.einsum('bcsgn,bcgrs,bcsgrp->bcgrpn', Bc, decay_end, xdt)
    chunk_decay = jnp.exp(acum[..., -1])

    def step(h, inp):
        dec, st = inp
        return dec[..., None, None] * h + st, h

    h_last, h_prev = lax.scan(step, h0, (jnp.moveaxis(chunk_decay, 1, 0), jnp.moveaxis(states, 1, 0)))
    h_prev = jnp.moveaxis(h_prev, 0, 1)
    y_off = jnp.einsum('bclgn,bcgrpn,bcgrl->bclgrp', Cc, h_prev, jnp.exp(acum))
    y = (y_diag + y_off).reshape(b, L, SSD_HEADS, SSD_HEADDIM)
    return y, h_last


def ssd_mixer(h_ctx, h_lat, in_w, conv_w, conv_b, dt_bias, A_log, D_skip, norm_w, out_w, ctx_out):
    def project(u):
        b, L = u.shape[:2]
        z, xbc, dtr = jnp.split(u @ in_w, [D_INNER, D_INNER + CONV_DIM], axis=-1)
        xbc = jax.nn.silu(dwconv_centred(xbc, conv_w, conv_b))
        xs, Bm, Cm = jnp.split(xbc, [D_INNER, D_INNER + SSD_GROUPS * D_STATE], axis=-1)
        dt = jax.nn.softplus(dtr.astype(jnp.float32).reshape(b, L, 2, SSD_HEADS) + dt_bias)
        return (z, xs.reshape(b, L, SSD_HEADS, SSD_HEADDIM),
                Bm.reshape(b, L, SSD_GROUPS, D_STATE), Cm.reshape(b, L, SSD_GROUPS, D_STATE), dt)

    zc, xc, Bc, Cc, dtc = project(h_ctx)
    zl, xl, Bl, Cl, dtl = project(h_lat)
    A = -jnp.exp(A_log.astype(jnp.float32))
    b = h_lat.shape[0]
    h0 = jnp.zeros((b, SSD_GROUPS, SSD_HPG, SSD_HEADDIM, D_STATE), jnp.float32)
    flip = lambda t: jnp.flip(t, axis=1)
    yc_f, hc_f = ssd_chunked(xc, dtc[:, :, 0], A[0], Bc, Cc, h0)
    yl_f, _ = ssd_chunked(xl, dtl[:, :, 0], A[0], Bl, Cl, hc_f)
    yc_b, hc_b = ssd_chunked(flip(xc), flip(dtc[:, :, 1]), A[1], flip(Bc), flip(Cc), h0)
    yl_b, _ = ssd_chunked(flip(xl), flip(dtl[:, :, 1]), A[1], flip(Bl), flip(Cl), hc_b)

    def finish(y_f, y_b, xs, z):
        bb, L = xs.shape[:2]
        y = y_f + flip(y_b) + D_skip[:, None] * xs
        y = y.reshape(bb, L, D_INNER).astype(z.dtype)
        return rmsnorm(y * jax.nn.silu(z), norm_w) @ out_w

    out_lat = finish(yl_f, yl_b, xl, zl)
    out_ctx = finish(yc_f, yc_b, xc, zc) if ctx_out else None
    return out_ctx, out_lat


def rope_2d(t, cos, sin):
    half = t.shape[-1] // 2
    t1, t2 = t[..., :half], t[..., half:]
    return jnp.concatenate([t1 * cos - t2 * sin, t2 * cos + t1 * sin], axis=-1)


def attn_mixer(h_ctx, h_lat, qkv_w, q_norm_w, k_norm_w, out_w, cos, sin, ctx_out):
    scale = 1.0 / math.sqrt(HEAD_DIM)

    def project(u):
        b, L = u.shape[:2]
        q, k, v = jnp.split(u @ qkv_w, [D_ATTN, D_ATTN + N_KV_HEADS * HEAD_DIM], axis=-1)
        q = rmsnorm(q.reshape(b, L, N_KV_HEADS, Q_PER_KV, HEAD_DIM), q_norm_w)
        k = rmsnorm(k.reshape(b, L, N_KV_HEADS, HEAD_DIM), k_norm_w)
        return q, k, v.reshape(b, L, N_KV_HEADS, HEAD_DIM)

    qc, kc, vc = project(h_ctx)
    ql, kl, vl = project(h_lat)
    cos = cos.astype(ql.dtype)
    sin = sin.astype(ql.dtype)
    ql = rope_2d(ql, cos[:, None, None], sin[:, None, None])
    kl = rope_2d(kl, cos[:, None], sin[:, None])
    k_all = jnp.concatenate([kl, kc], axis=1)
    v_all = jnp.concatenate([vl, vc], axis=1)

    def attend(q_blk, k, v):
        s = jnp.einsum('bqkgd,bskd->bkgqs', q_blk, k, preferred_element_type=jnp.float32) * scale
        p = jax.nn.softmax(s, axis=-1).astype(v.dtype)
        return jnp.einsum('bkgqs,bskd->bqkgd', p, v)

    b, S = h_lat.shape[:2]
    nb = S // Q_BLOCK
    qb = jnp.moveaxis(ql.reshape(b, nb, Q_BLOCK, N_KV_HEADS, Q_PER_KV, HEAD_DIM), 1, 0)
    o = lax.map(lambda q_blk: attend(q_blk, k_all, v_all), qb)
    out_lat = jnp.moveaxis(o, 0, 1).reshape(b, S, D_ATTN) @ out_w
    out_ctx = None
    if ctx_out:
        bc, Lc = h_ctx.shape[:2]
        out_ctx = attend(qc, kc, vc).reshape(bc, Lc, D_ATTN) @ out_w
    return out_ctx, out_lat


def moe_ec(h, router_w, w1, w3, w2):
    b, T, _ = h.shape
    cap = EC_CAPACITY * T // N_EXPERTS
    logits = jnp.einsum('btd,de->bte', h, router_w, preferred_element_type=jnp.float32)
    aff = jax.nn.softmax(logits, axis=-1)
    gate, idx = lax.top_k(jnp.swapaxes(aff, 1, 2), cap)
    bidx = jnp.arange(b)[:, None, None]
    xg = h[bidx, idx]
    a = jnp.einsum('becd,edf->becf', xg, w1)
    g = jnp.einsum('becd,edf->becf', xg, w3)
    y = jnp.einsum('becf,efd->becd', jax.nn.silu(a) * g, w2)
    y = y * gate[..., None].astype(y.dtype)
    return jnp.zeros_like(h).at[bidx, idx].add(y.astype(h.dtype))


def setup_inputs(seed: int = 0) -> dict:
    key = jax.random.key(seed)
    ks = jax.random.split(key, 26)
    f32 = jnp.float32
    nrm = lambda k, shape, s: jax.random.normal(k, shape, f32) * s
    dt0 = jnp.exp(jax.random.uniform(ks[11], (N_SSD_LAYERS, 2, SSD_HEADS), f32, math.log(1e-3), math.log(1e-1)))
    return {
        "x": nrm(ks[0], (BATCH, SEQ, D_MODEL), 1.0),
        "c": nrm(ks[1], (BATCH, D_MODEL), 1.0),
        "ctx": nrm(ks[2], (BATCH, CTX_LEN, D_MODEL), 1.0),
        "c_ctx": nrm(ks[3], (D_MODEL,), 1.0),
        "ada_w": nrm(ks[4], (DEPTH, D_MODEL, 6 * D_MODEL), D_MODEL ** -0.5),
        "ada_b": nrm(ks[5], (DEPTH, 6 * D_MODEL), 0.02),
        "norm1_w": 1.0 + nrm(ks[6], (DEPTH, D_MODEL), 0.02),
        "norm2_w": 1.0 + nrm(ks[7], (DEPTH, D_MODEL), 0.02),
        "ssd_in_w": nrm(ks[8], (N_SSD_LAYERS, D_MODEL, SSD_IN_DIM), D_MODEL ** -0.5),
        "ssd_conv_w": nrm(ks[9], (N_SSD_LAYERS, CONV_W, CONV_DIM), CONV_W ** -0.5),
        "ssd_conv_b": nrm(ks[10], (N_SSD_LAYERS, CONV_DIM), 0.02),
        "ssd_dt_bias": dt0 + jnp.log(-jnp.expm1(-dt0)),
        "ssd_A_log": jnp.log(jax.random.uniform(ks[12], (N_SSD_LAYERS, 2, SSD_HEADS), f32, 1.0, 16.0)),
        "ssd_D": 1.0 + nrm(ks[13], (N_SSD_LAYERS, SSD_HEADS), 0.1),
        "ssd_norm_w": 1.0 + nrm(ks[14], (N_SSD_LAYERS, D_INNER), 0.02),
        "ssd_out_w": nrm(ks[15], (N_SSD_LAYERS, D_INNER, D_MODEL), D_INNER ** -0.5),
        "attn_qkv_w": nrm(ks[16], (N_ATTN_LAYERS, D_MODEL, QKV_DIM), D_MODEL ** -0.5),
        "attn_q_norm_w": 1.0 + nrm(ks[17], (N_ATTN_LAYERS, HEAD_DIM), 0.02),
        "attn_k_norm_w": 1.0 + nrm(ks[18], (N_ATTN_LAYERS, HEAD_DIM), 0.02),
        "attn_out_w": nrm(ks[19], (N_ATTN_LAYERS, D_ATTN, D_MODEL), D_ATTN ** -0.5),
        "moe_router_w": nrm(ks[20], (DEPTH, D_MODEL, N_EXPERTS), D_MODEL ** -0.5),
        "moe_w1": nrm(ks[21], (DEPTH, N_EXPERTS, D_MODEL, D_EXPERT), D_MODEL ** -0.5),
        "moe_w3": nrm(ks[22], (DEPTH, N_EXPERTS, D_MODEL, D_EXPERT), D_MODEL ** -0.5),
        "moe_w2": nrm(ks[23], (DEPTH, N_EXPERTS, D_EXPERT, D_MODEL), D_EXPERT ** -0.5),
        "final_norm_w": 1.0 + nrm(ks[24], (D_MODEL,), 0.02),
    }


def reference(x, c, ctx, c_ctx, ada_w, ada_b, norm1_w, norm2_w, ssd_in_w, ssd_conv_w, ssd_conv_b,
              ssd_dt_bias, ssd_A_log, ssd_D, ssd_norm_w, ssd_out_w, attn_qkv_w, attn_q_norm_w,
              attn_k_norm_w, attn_out_w, moe_router_w, moe_w1, moe_w3, moe_w2, final_norm_w):
    S = x.shape[1]
    rows = S // GRID_W
    row = jnp.repeat(jnp.arange(rows), GRID_W, total_repeat_length=S).astype(jnp.float32)
    col = jnp.tile(jnp.arange(GRID_W), rows).astype(jnp.float32)
    inv_freq = ROPE_THETA ** (-jnp.arange(0, ROPE_AXIS_DIM, 2, dtype=jnp.float32) / ROPE_AXIS_DIM)
    ang = jnp.concatenate([row[:, None] * inv_freq, col[:, None] * inv_freq], axis=-1)
    cos, sin = jnp.cos(ang), jnp.sin(ang)

    silu_c = jax.nn.silu(c)
    silu_cc = jax.nn.silu(c_ctx)
    for i in range(DEPTH):
        last = i == DEPTH - 1
        j = i // N_MIXERS
        sh1l, sc1l, g1l, sh2l, sc2l, g2l = [m[:, None, :] for m in jnp.split(silu_c @ ada_w[i] + ada_b[i], 6, axis=-1)]
        sh1c, sc1c, g1c, sh2c, sc2c, g2c = jnp.split(silu_cc @ ada_w[i] + ada_b[i], 6, axis=-1)
        h_lat = rmsnorm(x, norm1_w[i]) * (1.0 + sc1l) + sh1l
        h_ctx = rmsnorm(ctx, norm1_w[i]) * (1.0 + sc1c) + sh1c
        if i % N_MIXERS == 0:
            o_ctx, o_lat = ssd_mixer(h_ctx, h_lat, ssd_in_w[j], ssd_conv_w[j], ssd_conv_b[j], ssd_dt_bias[j],
                                     ssd_A_log[j], ssd_D[j], ssd_norm_w[j], ssd_out_w[j], not last)
        else:
            o_ctx, o_lat = attn_mixer(h_ctx, h_lat, attn_qkv_w[j], attn_q_norm_w[j], attn_k_norm_w[j],
                                      attn_out_w[j], cos, sin, not last)
        x = x + g1l * o_lat
        x = x + g2l * moe_ec(rmsnorm(x, norm2_w[i]) * (1.0 + sc2l) + sh2l,
                             moe_router_w[i], moe_w1[i], moe_w3[i], moe_w2[i])
        if not last:
            ctx = ctx + g1c * o_ctx
            ctx = ctx + g2c * moe_ec(rmsnorm(ctx, norm2_w[i]) * (1.0 + sc2c) + sh2c,
                                     moe_router_w[i], moe_w1[i], moe_w3[i], moe_w2[i])
    return rmsnorm(x, final_norm_w)
```

```python
import functools
import math

import numpy as np
import jax
import jax.numpy as jnp
from jax import lax
from jax.experimental import pallas as pl
from jax.experimental.pallas import tpu as pltpu

F32 = jnp.float32
BF16 = jnp.bfloat16
HIGHEST = lax.Precision.HIGHEST

D_MODEL = 1024
GRID_W = 64
EPS = 1e-6
D_INNER = 2048
SSD_HEADDIM = 64
SSD_HEADS = 32
SSD_GROUPS = 8
SSD_HPG = 4
D_STATE = 128
CONV_W = 5
SSD_CHUNK = 128
CONV_DIM = D_INNER + 2 * SSD_GROUPS * D_STATE
SSD_IN_DIM = 2 * D_INNER + 2 * SSD_GROUPS * D_STATE + 2 * SSD_HEADS
SSD_IN_PAD = 6272
HEAD_DIM = 64
N_Q_HEADS = 16
N_KV_HEADS = 4
Q_PER_KV = 4
D_ATTN = 1024
KV_DIM = N_KV_HEADS * HEAD_DIM
QKV_DIM = D_ATTN + 2 * KV_DIM
ROPE_THETA = 10000.0
N_EXPERTS = 16
EC_CAPACITY = 2

LANES = 128
SUBLANES = 8
VMEM_LIMIT_BYTES = 56 * 1024 * 1024


def _cparams(*sem):
    return pltpu.CompilerParams(dimension_semantics=sem, vmem_limit_bytes=VMEM_LIMIT_BYTES)


def _sigmoid(x):
    return 1.0 / (1.0 + jnp.exp(-x))


def _silu(x):
    return x * _sigmoid(x)


def _ada_kernel(c_ref, w_ref, b_ref, o_ref):
    a = _silu(c_ref[...])
    o_ref[...] = jnp.dot(a, w_ref[...], precision=HIGHEST, preferred_element_type=F32) + b_ref[...]


def ada_modulation(c_all, ada_w, ada_b):
    depth, d, n = ada_w.shape
    m = c_all.shape[0]
    tn = 1536
    return pl.pallas_call(
        _ada_kernel,
        grid=(depth, n // tn),
        in_specs=[pl.BlockSpec((m, d), lambda l, j: (0, 0)),
                  pl.BlockSpec((None, d, tn), lambda l, j: (l, 0, j)),
                  pl.BlockSpec((None, 1, tn), lambda l, j: (l, 0, j))],
        out_specs=pl.BlockSpec((None, m, tn), lambda l, j: (l, 0, j)),
        out_shape=jax.ShapeDtypeStruct((depth, m, n), F32),
        compiler_params=_cparams("parallel", "parallel"),
        name="ada_modulation",
    )(c_all, ada_w, ada_b.reshape(depth, 1, n))


def _norm_mod_mm_kernel(x_ref, wm_ref, sh_ref, w_ref, o_ref, a_scr):
    @pl.when(pl.program_id(1) == 0)
    def _():
        x = x_ref[...]
        ms = jnp.mean(x * x, axis=-1, keepdims=True)
        a_scr[...] = (x * lax.rsqrt(ms + EPS) * wm_ref[...] + sh_ref[...]).astype(BF16)

    o_ref[...] = jnp.dot(a_scr[...], w_ref[...], preferred_element_type=F32).astype(o_ref.dtype)


def norm_mod_mm(x2d, wm, sh, w, rows_per_group, tm, tn, name):
    m, k = x2d.shape
    n = w.shape[1]
    tiles_per_group = rows_per_group // tm
    return pl.pallas_call(
        _norm_mod_mm_kernel,
        grid=(m // tm, n // tn),
        in_specs=[pl.BlockSpec((tm, k), lambda i, j: (i, 0)),
                  pl.BlockSpec((None, 1, k), lambda i, j: (i // tiles_per_group, 0, 0)),
                  pl.BlockSpec((None, 1, k), lambda i, j: (i // tiles_per_group, 0, 0)),
                  pl.BlockSpec((k, tn), lambda i, j: (0, j))],
        out_specs=pl.BlockSpec((tm, tn), lambda i, j: (i, j)),
        out_shape=jax.ShapeDtypeStruct((m, n), F32),
        scratch_shapes=[pltpu.VMEM((tm, k), BF16)],
        compiler_params=_cparams("parallel", "arbitrary"),
        name=name,
    )(x2d, wm, sh, w)


def _res_mm_kernel(a_ref, w_ref, res_ref, gate_ref, o_ref):
    acc = jnp.dot(a_ref[...], w_ref[...], preferred_element_type=F32)
    o_ref[...] = res_ref[...] + gate_ref[...] * acc


def res_mm(a, w, res, gate, rows_per_group, tm, name):
    m, k = a.shape
    n = w.shape[1]
    tiles_per_group = rows_per_group // tm
    return pl.pallas_call(
        _res_mm_kernel,
        grid=(m // tm,),
        in_specs=[pl.BlockSpec((tm, k), lambda i: (i, 0)),
                  pl.BlockSpec((k, n), lambda i: (0, 0)),
                  pl.BlockSpec((tm, n), lambda i: (i, 0)),
                  pl.BlockSpec((None, 1, n), lambda i: (i // tiles_per_group, 0, 0))],
        out_specs=pl.BlockSpec((tm, n), lambda i: (i, 0)),
        out_shape=jax.ShapeDtypeStruct((m, n), F32),
        compiler_params=_cparams("parallel"),
        name=name,
    )(a, w, res, gate)


def _ssd_out_kernel(y_ref, z_ref, nw_ref, w_ref, res_ref, gate_ref, o_ref):
    v = y_ref[...] * _silu(z_ref[...])
    ms = jnp.mean(v * v, axis=-1, keepdims=True)
    a = (v * lax.rsqrt(ms + EPS) * nw_ref[...]).astype(BF16)
    acc = jnp.dot(a, w_ref[...], preferred_element_type=F32)
    o_ref[...] = res_ref[...] + gate_ref[...] * acc


def ssd_out_mm(y, proj, norm_w, w, res, gate, rows_per_group, tm, name):
    m, k = y.shape
    n = w.shape[1]
    tiles_per_group = rows_per_group // tm
    return pl.pallas_call(
        _ssd_out_kernel,
        grid=(m // tm,),
        in_specs=[pl.BlockSpec((tm, k), lambda i: (i, 0)),
                  pl.BlockSpec((tm, k), lambda i: (i, 0)),
                  pl.BlockSpec((1, k), lambda i: (0, 0)),
                  pl.BlockSpec((k, n), lambda i: (0, 0)),
                  pl.BlockSpec((tm, n), lambda i: (i, 0)),
                  pl.BlockSpec((None, 1, n), lambda i: (i // tiles_per_group, 0, 0))],
        out_specs=pl.BlockSpec((tm, n), lambda i: (i, 0)),
        out_shape=jax.ShapeDtypeStruct((m, n), F32),
        compiler_params=_cparams("parallel"),
        name=name,
    )(y, proj, norm_w, w, res, gate)


def _ssd_prep_kernel(dtr_ref, bias_ref, a_ref, scol_ref, srow_ref, col_ref, row_ref):
    seq = dtr_ref.shape[0]
    x = dtr_ref[...] + bias_ref[...]
    dt = jnp.maximum(x, 0.0) + jnp.log1p(jnp.exp(-jnp.abs(x)))
    a = dt * a_ref[...]
    t_i = lax.broadcasted_iota(jnp.int32, (SSD_CHUNK, SSD_CHUNK), 0)
    s_i = lax.broadcasted_iota(jnp.int32, (SSD_CHUNK, SSD_CHUNK), 1)
    tri_lo = (s_i <= t_i).astype(F32)
    tri_up = (s_i >= t_i).astype(F32)
    fwd_lane = lax.broadcasted_iota(jnp.int32, (SSD_CHUNK, LANES), 1) < SSD_HEADS
    chunks = []
    for c in range(seq // SSD_CHUNK):
        ac = a[c * SSD_CHUNK:(c + 1) * SSD_CHUNK]
        cf = jnp.dot(tri_lo, ac, precision=HIGHEST, preferred_element_type=F32)
        cb = jnp.dot(tri_up, ac, precision=HIGHEST, preferred_element_type=F32)
        chunks.append(jnp.where(fwd_lane, cf, cb))
    cum = jnp.concatenate(chunks, axis=0)
    both = jnp.concatenate([dt, cum], axis=1)
    col_ref[...] = jnp.dot(both, scol_ref[...], precision=HIGHEST, preferred_element_type=F32)
    row_ref[...] = lax.dot_general(srow_ref[...], cum, (((1,), (1,)), ((), ())),
                                   precision=HIGHEST, preferred_element_type=F32)


def _ssd_select_tables():
    scol = np.zeros((2 * LANES, SSD_GROUPS * LANES), np.float32)
    srow = np.zeros((SSD_GROUPS * SUBLANES, LANES), np.float32)
    for g in range(SSD_GROUPS):
        for d in range(2):
            for r in range(SSD_HPG):
                lane = d * SSD_HEADS + SSD_HPG * g + r
                scol[lane, g * LANES + d * SSD_HPG + r] = 1.0
                scol[LANES + lane, g * LANES + 2 * SSD_HPG + d * SSD_HPG + r] = 1.0
                srow[g * SUBLANES + d * SSD_HPG + r, lane] = 1.0
    return jnp.asarray(scol), jnp.asarray(srow)


def ssd_prep(proj3, dt_bias, a_neg, name):
    b, seq, _ = proj3.shape
    scol, srow = _ssd_select_tables()
    dt_blk = (2 * D_INNER + 2 * SSD_GROUPS * D_STATE) // LANES
    return pl.pallas_call(
        _ssd_prep_kernel,
        grid=(b,),
        in_specs=[pl.BlockSpec((None, seq, LANES), lambda i: (i, 0, dt_blk)),
                  pl.BlockSpec((1, LANES), lambda i: (0, 0)),
                  pl.BlockSpec((1, LANES), lambda i: (0, 0)),
                  pl.BlockSpec(scol.shape, lambda i: (0, 0)),
                  pl.BlockSpec(srow.shape, lambda i: (0, 0))],
        out_specs=[pl.BlockSpec((None, seq, SSD_GROUPS * LANES), lambda i: (i, 0, 0)),
                   pl.BlockSpec((None, SSD_GROUPS * SUBLANES, seq), lambda i: (i, 0, 0))],
        out_shape=[jax.ShapeDtypeStruct((b, seq, SSD_GROUPS * LANES), F32),
                   jax.ShapeDtypeStruct((b, SSD_GROUPS * SUBLANES, seq), F32)],
        compiler_params=_cparams("parallel"),
        name=name,
    )(proj3, dt_bias, a_neg, scol, srow)


CONV_HALO = SUBLANES


def _ssd_scan_kernel(xc_ref, bc_ref, cc_ref, xl_ref, bl_ref, cl_ref,
                     cwx_ref, cwb_ref, cwc_ref, cbx_ref, cbb_ref, cbc_ref,
                     colc_ref, rowc_ref, coll_ref, rowl_ref, dsk_ref,
                     yc_ref, yl_ref,
                     pad_scr, xsc, bsc, csc, xsl, bsl, csl, h_scr):
    q = SSD_CHUNK
    width = SSD_HPG * SSD_HEADDIM

    def conv_silu(raw_ref, w_ref, b_ref, out_ref):
        seq, ch = raw_ref.shape
        zeros = jnp.zeros((CONV_HALO, ch), F32)
        pad_scr[0:CONV_HALO, 0:ch] = zeros
        pad_scr[CONV_HALO + seq:2 * CONV_HALO + seq, 0:ch] = zeros

        def copy_body(c, carry):
            base = pl.multiple_of(c * q, q)
            dst = pl.multiple_of(base + CONV_HALO, SUBLANES)
            pad_scr[pl.ds(dst, q), 0:ch] = raw_ref[pl.ds(base, q), :]
            return carry

        lax.fori_loop(0, seq // q, copy_body, 0)
        w = w_ref[...]
        bias = b_ref[...]

        def conv_body(c, carry):
            base = pl.multiple_of(c * q, q)
            win = pad_scr[pl.ds(base, q + 2 * CONV_HALO), 0:ch]
            acc = bias
            for k in range(CONV_W):
                off = CONV_HALO - CONV_W // 2 + k
                acc = acc + win[off:off + q] * w[k:k + 1]
            out_ref[pl.ds(base, q), :] = _silu(acc)
            return carry

        lax.fori_loop(0, seq // q, conv_body, 0)

    conv_silu(xc_ref, cwx_ref, cbx_ref, xsc)
    conv_silu(bc_ref, cwb_ref, cbb_ref, bsc)
    conv_silu(cc_ref, cwc_ref, cbc_ref, csc)
    conv_silu(xl_ref, cwx_ref, cbx_ref, xsl)
    conv_silu(bl_ref, cwb_ref, cbb_ref, bsl)
    conv_silu(cl_ref, cwc_ref, cbc_ref, csl)

    t_i = lax.broadcasted_iota(jnp.int32, (q, q), 0)
    s_i = lax.broadcasted_iota(jnp.int32, (q, q), 1)
    head_of_lane = lax.broadcasted_iota(jnp.int32, (q, width), 1) // SSD_HEADDIM
    dsk = dsk_ref[...]

    def per_head(parts):
        out = parts[SSD_HPG - 1]
        for r in range(SSD_HPG - 2, -1, -1):
            out = jnp.where(head_of_lane == r, parts[r], out)
        return out

    def run_seq(xs_ref, b_ref, c_ref, col_ref, row_ref, y_ref, d):
        nchunks = xs_ref.shape[0] // q
        mask = (s_i <= t_i) if d == 0 else (s_i >= t_i)
        edge = q - 1 if d == 0 else 0

        def body(i, carry):
            c = i if d == 0 else nchunks - 1 - i
            base = pl.multiple_of(c * q, q)
            xs = xs_ref[pl.ds(base, q), :]
            bq_t = b_ref[pl.ds(base, q), :].T.astype(BF16)
            cq = c_ref[pl.ds(base, q), :].astype(BF16)
            cp = col_ref[pl.ds(base, q), :]
            rp = row_ref[:, pl.ds(base, q)]
            cb = jnp.dot(cq, bq_t, preferred_element_type=F32)
            dt_cols = [cp[:, SSD_HPG * d + r:SSD_HPG * d + r + 1] for r in range(SSD_HPG)]
            cum_cols = [cp[:, 2 * SSD_HPG + SSD_HPG * d + r:2 * SSD_HPG + SSD_HPG * d + r + 1]
                        for r in range(SSD_HPG)]
            dt_e = per_head(dt_cols)
            cum_e = per_head(cum_cols)
            xdt = xs * dt_e
            ms = []
            for r in range(SSD_HPG):
                row = rp[SSD_HPG * d + r:SSD_HPG * d + r + 1, :]
                seg = jnp.exp(jnp.where(mask, cum_cols[r] - row, -1e30))
                ms.append((cb * seg).astype(BF16))
            stacked = jnp.dot(jnp.concatenate(ms, axis=0), xdt.astype(BF16),
                              preferred_element_type=F32)
            y = per_head([stacked[r * q:(r + 1) * q] for r in range(SSD_HPG)])
            h_t = h_scr[...]
            y = y + jnp.dot(cq, h_t.astype(BF16), preferred_element_type=F32) * jnp.exp(cum_e)
            cum_edge = cum_e[edge:edge + 1, :]
            decay_to_edge = jnp.exp(cum_edge - cum_e)
            h_scr[...] = h_t * jnp.exp(cum_edge) + jnp.dot(
                bq_t, (xdt * decay_to_edge).astype(BF16), preferred_element_type=F32)
            if d == 0:
                y_ref[pl.ds(base, q), :] = y + dsk * xs
            else:
                y_ref[pl.ds(base, q), :] += y
            return carry

        lax.fori_loop(0, nchunks, body, 0)

    for d in range(2):
        h_scr[...] = jnp.zeros(h_scr.shape, F32)
        run_seq(xsc, bsc, csc, colc_ref, rowc_ref, yc_ref, d)
        run_seq(xsl, bsl, csl, coll_ref, rowl_ref, yl_ref, d)


def ssd_scan(proj_c, proj_l, conv_w, conv_b, col_c, row_c, col_l, row_l, dskip_e):
    b, lc, _ = proj_c.shape
    ll = proj_l.shape[1]
    width = SSD_HPG * SSD_HEADDIM
    x0 = D_INNER // width
    b0 = (2 * D_INNER) // D_STATE
    c0 = b0 + SSD_GROUPS
    wx0, wb0, wc0 = 0, D_INNER // D_STATE, D_INNER // D_STATE + SSD_GROUPS

    def seq_specs(seq):
        return [pl.BlockSpec((None, seq, width), lambda i, g: (i, 0, x0 + g)),
                pl.BlockSpec((None, seq, D_STATE), lambda i, g: (i, 0, b0 + g)),
                pl.BlockSpec((None, seq, D_STATE), lambda i, g: (i, 0, c0 + g))]

    def par_specs(rows):
        return [pl.BlockSpec((rows, width), lambda i, g: (0, wx0 + g)),
                pl.BlockSpec((rows, D_STATE), lambda i, g: (0, wb0 + g)),
                pl.BlockSpec((rows, D_STATE), lambda i, g: (0, wc0 + g))]

    def pack_specs(seq):
        return [pl.BlockSpec((None, seq, LANES), lambda i, g: (i, 0, g)),
                pl.BlockSpec((None, SUBLANES, seq), lambda i, g: (i, g, 0))]

    return pl.pallas_call(
        _ssd_scan_kernel,
        grid=(b, SSD_GROUPS),
        in_specs=(seq_specs(lc) + seq_specs(ll) + par_specs(CONV_W) + par_specs(1)
                  + pack_specs(lc) + pack_specs(ll)
                  + [pl.BlockSpec((None, 1, width), lambda i, g: (g, 0, 0))]),
        out_specs=[pl.BlockSpec((None, lc, width), lambda i, g: (i, 0, g)),
                   pl.BlockSpec((None, ll, width), lambda i, g: (i, 0, g))],
        out_shape=[jax.ShapeDtypeStruct((b, lc, D_INNER), F32),
                   jax.ShapeDtypeStruct((b, ll, D_INNER), F32)],
        scratch_shapes=[pltpu.VMEM((ll + 2 * CONV_HALO, width), F32),
                        pltpu.VMEM((lc, width), F32), pltpu.VMEM((lc, D_STATE), F32),
                        pltpu.VMEM((lc, D_STATE), F32),
                        pltpu.VMEM((ll, width), F32), pltpu.VMEM((ll, D_STATE), F32),
                        pltpu.VMEM((ll, D_STATE), F32),
                        pltpu.VMEM((D_STATE, width), F32)],
        compiler_params=_cparams("parallel", "parallel"),
        name="ssd_scan",
    )(proj_c, proj_c, proj_c, proj_l, proj_l, proj_l,
      conv_w, conv_w, conv_w, conv_b, conv_b, conv_b,
      col_c, row_c, col_l, row_l, dskip_e)


def _qk_prep_kernel(qkv_ref, gmat_ref, qw_ref, kw_ref, rope_c_ref, rope_a_ref, rope_b_ref, rep_ref,
                    q_ref, k_ref, v_ref, *, use_rope):
    blk = Q_PER_KV * HEAD_DIM
    gmat = gmat_ref[...]

    def norm_rope(x, w):
        ms = jnp.dot(x * x, gmat, precision=HIGHEST, preferred_element_type=F32)
        xn = x * lax.rsqrt(ms + EPS) * w
        if use_rope:
            half = HEAD_DIM // 2
            xn = (xn * rope_c_ref[...] + pltpu.roll(xn, blk - half, 1) * rope_a_ref[...]
                  + pltpu.roll(xn, half, 1) * rope_b_ref[...])
        return xn

    scale = 1.0 / math.sqrt(HEAD_DIM)
    for j in range(D_ATTN // blk):
        xq = norm_rope(qkv_ref[:, j * blk:(j + 1) * blk], qw_ref[...])
        q_ref[:, j * blk:(j + 1) * blk] = (xq * scale).astype(BF16)
    xk = norm_rope(qkv_ref[:, D_ATTN:D_ATTN + KV_DIM], kw_ref[...]).astype(BF16)
    xv = qkv_ref[:, D_ATTN + KV_DIM:D_ATTN + 2 * KV_DIM].astype(BF16)
    for kv in range(N_KV_HEADS):
        rep = rep_ref[kv]
        k_ref[:, kv * blk:(kv + 1) * blk] = jnp.dot(xk, rep, preferred_element_type=F32).astype(BF16)
        v_ref[:, kv * blk:(kv + 1) * blk] = jnp.dot(xv, rep, preferred_element_type=F32).astype(BF16)


def _attn_tables(seq):
    rows = seq // GRID_W
    row = np.repeat(np.arange(rows), GRID_W).astype(np.float32)
    col = np.tile(np.arange(GRID_W), rows).astype(np.float32)
    axis_dim = HEAD_DIM // 2
    inv_freq = jnp.asarray(ROPE_THETA, F32) ** (-jnp.arange(0, axis_dim, 2, dtype=F32) / axis_dim)
    ang = jnp.concatenate([jnp.asarray(row)[:, None] * inv_freq, jnp.asarray(col)[:, None] * inv_freq], axis=-1)
    cos, sin = jnp.cos(ang), jnp.sin(ang)
    zero = jnp.zeros_like(sin)
    reps = Q_PER_KV
    rope_c = jnp.tile(jnp.concatenate([cos, cos], axis=-1), (1, reps))
    rope_a = jnp.tile(jnp.concatenate([-sin, zero], axis=-1), (1, reps))
    rope_b = jnp.tile(jnp.concatenate([zero, sin], axis=-1), (1, reps))
    return rope_c, rope_a, rope_b


def _attn_consts():
    blk = Q_PER_KV * HEAD_DIM
    head = np.arange(blk) // HEAD_DIM
    gmat = (head[:, None] == head[None, :]).astype(np.float32) / HEAD_DIM
    rep = np.zeros((N_KV_HEADS, KV_DIM, blk), np.float32)
    for kv in range(N_KV_HEADS):
        for j in range(blk):
            rep[kv, kv * HEAD_DIM + j % HEAD_DIM, j] = 1.0
    return jnp.asarray(gmat), jnp.asarray(rep, dtype=BF16)


def qk_prep(qkv, q_norm_w, k_norm_w, tables, seq, use_rope, tl, name):
    m = qkv.shape[0]
    blk = Q_PER_KV * HEAD_DIM
    gmat, rep = _attn_consts()
    qw = jnp.tile(q_norm_w.reshape(1, HEAD_DIM), (1, Q_PER_KV))
    kw = jnp.tile(k_norm_w.reshape(1, HEAD_DIM), (1, N_KV_HEADS))
    tiles_per_seq = seq // tl
    tab_spec = pl.BlockSpec((tl, blk), lambda i: (i % tiles_per_seq, 0))
    out_w = N_KV_HEADS * blk
    return pl.pallas_call(
        functools.partial(_qk_prep_kernel, use_rope=use_rope),
        grid=(m // tl,),
        in_specs=[pl.BlockSpec((tl, QKV_DIM), lambda i: (i, 0)),
                  pl.BlockSpec((blk, blk), lambda i: (0, 0)),
                  pl.BlockSpec((1, blk), lambda i: (0, 0)),
                  pl.BlockSpec((1, blk), lambda i: (0, 0)),
                  tab_spec, tab_spec, tab_spec,
                  pl.BlockSpec(rep.shape, lambda i: (0, 0, 0))],
        out_specs=[pl.BlockSpec((tl, D_ATTN), lambda i: (i, 0)),
                   pl.BlockSpec((tl, out_w), lambda i: (i, 0)),
                   pl.BlockSpec((tl, out_w), lambda i: (i, 0))],
        out_shape=[jax.ShapeDtypeStruct((m, D_ATTN), BF16),
                   jax.ShapeDtypeStruct((m, out_w), BF16),
                   jax.ShapeDtypeStruct((m, out_w), BF16)],
        compiler_params=_cparams("parallel"),
        name=name,
    )(qkv, gmat, qw, kw, *tables, rep)


def _attn_kernel(q_ref, kl_ref, vl_ref, kc_ref, vc_ref, o_ref):
    tq = q_ref.shape[0]
    blk = Q_PER_KV * HEAD_DIM
    head_of_lane = lax.broadcasted_iota(jnp.int32, (tq, blk), 1) // HEAD_DIM
    nt = (((1,), (1,)), ((), ()))
    for kv in range(N_KV_HEADS):
        cols = slice(kv * blk, (kv + 1) * blk)
        q4 = q_ref[:, cols]
        zero = jnp.zeros_like(q4)
        qs = jnp.concatenate([jnp.where(head_of_lane == g, q4, zero) for g in range(Q_PER_KV)], axis=0)
        s_l = lax.dot_general(qs, kl_ref[:, cols], nt, preferred_element_type=F32)
        s_c = lax.dot_general(qs, kc_ref[:, cols], nt, preferred_element_type=F32)
        m = jnp.maximum(jnp.max(s_l, axis=-1, keepdims=True), jnp.max(s_c, axis=-1, keepdims=True))
        p_l = jnp.exp(s_l - m)
        p_c = jnp.exp(s_c - m)
        denom = jnp.sum(p_l, axis=-1, keepdims=True) + jnp.sum(p_c, axis=-1, keepdims=True)
        o = (jnp.dot(p_l.astype(BF16), vl_ref[:, cols], preferred_element_type=F32)
             + jnp.dot(p_c.astype(BF16), vc_ref[:, cols], preferred_element_type=F32)) / denom
        out = o[(Q_PER_KV - 1) * tq:Q_PER_KV * tq]
        for g in range(Q_PER_KV - 2, -1, -1):
            out = jnp.where(head_of_lane == g, o[g * tq:(g + 1) * tq], out)
        o_ref[:, cols] = out.astype(BF16)


def attention(q, k_lat, v_lat, k_ctx, v_ctx, batch, seq, ctx_len, tq):
    width = q.shape[1]
    nq = seq // tq
    return pl.pallas_call(
        _attn_kernel,
        grid=(batch, nq),
        in_specs=[pl.BlockSpec((tq, width), lambda b, i: (b * nq + i, 0)),
                  pl.BlockSpec((seq, width), lambda b, i: (b, 0)),
                  pl.BlockSpec((seq, width), lambda b, i: (b, 0)),
                  pl.BlockSpec((ctx_len, width), lambda b, i: (b, 0)),
                  pl.BlockSpec((ctx_len, width), lambda b, i: (b, 0))],
        out_specs=pl.BlockSpec((tq, width), lambda b, i: (b * nq + i, 0)),
        out_shape=jax.ShapeDtypeStruct((batch * seq, width), BF16),
        compiler_params=_cparams("parallel", "parallel"),
        name="attention",
    )(q, k_lat, v_lat, k_ctx, v_ctx)


def _router_kernel(x_ref, wm_ref, sh_ref, rw_ref, hn_ref, pos_ref, gate_ref, *, cap):
    tokens = x_ref.shape[0]
    x = x_ref[...]
    ms = jnp.mean(x * x, axis=-1, keepdims=True)
    hn = x * lax.rsqrt(ms + EPS) * wm_ref[...] + sh_ref[...]
    hn_ref[...] = hn.astype(BF16)
    logits = lax.dot_general(rw_ref[...], hn, (((1,), (1,)), ((), ())),
                             precision=HIGHEST, preferred_element_type=F32)
    e = jnp.exp(logits - jnp.max(logits, axis=0, keepdims=True))
    aff = e / jnp.sum(e, axis=0, keepdims=True)
    gate_ref[...] = aff
    bits = pltpu.bitcast(aff, jnp.int32)

    def count(flags):
        return jnp.sum(jnp.where(flags, 1.0, 0.0), axis=1, keepdims=True)

    def search(i, thr):
        cand = thr | jnp.left_shift(jnp.int32(1), 30 - i)
        return jnp.where(count(bits >= cand) >= cap, cand, thr)

    thr = lax.fori_loop(0, 31, search, jnp.zeros((N_EXPERTS, 1), jnp.int32))
    above = bits > thr
    tied = bits == thr
    need = cap - count(above)

    t_i = lax.broadcasted_iota(jnp.int32, (LANES, LANES), 0)
    s_i = lax.broadcasted_iota(jnp.int32, (LANES, LANES), 1)
    tri = (t_i <= s_i).astype(BF16)

    def lane_cumsum(flags):
        carry = jnp.zeros((N_EXPERTS, 1), F32)
        parts = []
        for c in range(tokens // LANES):
            part = jnp.dot(flags[:, c * LANES:(c + 1) * LANES].astype(BF16), tri,
                           preferred_element_type=F32) + carry
            parts.append(part)
            carry = part[:, LANES - 1:LANES]
        return jnp.concatenate(parts, axis=1)

    tie_rank = lane_cumsum(jnp.where(tied, 1.0, 0.0))
    sel = above | (tied & (tie_rank <= need))
    slot = lane_cumsum(jnp.where(sel, 1.0, 0.0)) - 1.0
    pos_ref[...] = jnp.where(sel, slot, -1.0).astype(jnp.int32)


def moe_router(x3, wm, sh, router_wt, cap, name):
    b, tokens, d = x3.shape
    per_batch = wm.shape[0] > 1
    mod_spec = pl.BlockSpec((None, 1, d), (lambda i: (i, 0, 0)) if per_batch else (lambda i: (0, 0, 0)))
    return pl.pallas_call(
        functools.partial(_router_kernel, cap=cap),
        grid=(b,),
        in_specs=[pl.BlockSpec((None, tokens, d), lambda i: (i, 0, 0)),
                  mod_spec, mod_spec,
                  pl.BlockSpec((N_EXPERTS, d), lambda i: (0, 0))],
        out_specs=[pl.BlockSpec((None, tokens, d), lambda i: (i, 0, 0)),
                   pl.BlockSpec((None, N_EXPERTS, tokens), lambda i: (i, 0, 0)),
                   pl.BlockSpec((None, N_EXPERTS, tokens), lambda i: (i, 0, 0))],
        out_shape=[jax.ShapeDtypeStruct((b, tokens, d), BF16),
                   jax.ShapeDtypeStruct((b, N_EXPERTS, tokens), jnp.int32),
                   jax.ShapeDtypeStruct((b, N_EXPERTS, tokens), F32)],
        compiler_params=_cparams("parallel"),
        name=name,
    )(x3, wm, sh, router_wt)


def _expert_kernel(hn_ref, pos_ref, gate_ref, w1_ref, w3_ref, w2_ref, g2_ref, x_ref, fw_ref, o_ref,
                   *, cap, final_norm):
    nb, tokens, _ = hn_ref.shape
    e = pl.program_id(1)

    @pl.when(e == 0)
    def _():
        o_ref[...] = x_ref[...]

    slot_i = lax.broadcasted_iota(jnp.int32, (cap, tokens), 0)
    onehots, gathered, gates = [], [], []
    for n in range(nb):
        hit = slot_i == pos_ref[n]
        onehot = jnp.where(hit, 1.0, 0.0).astype(BF16)
        onehots.append(onehot)
        gates.append(jnp.sum(jnp.where(hit, gate_ref[n], 0.0), axis=1, keepdims=True))
        gathered.append(jnp.dot(onehot, hn_ref[n], preferred_element_type=F32).astype(BF16))
    xg = jnp.concatenate(gathered, axis=0) if nb > 1 else gathered[0]
    a = jnp.dot(xg, w1_ref[...], preferred_element_type=F32)
    g = jnp.dot(xg, w3_ref[...], preferred_element_type=F32)
    y = jnp.dot((_silu(a) * g).astype(BF16), w2_ref[...], preferred_element_type=F32)
    for n in range(nb):
        yn = (y[n * cap:(n + 1) * cap] * gates[n] * g2_ref[n]).astype(BF16)
        o_ref[n] += lax.dot_general(onehots[n], yn, (((0,), (0,)), ((), ())),
                                    preferred_element_type=F32)

    if final_norm:
        @pl.when(e == pl.num_programs(1) - 1)
        def _():
            for n in range(nb):
                v = o_ref[n]
                ms = jnp.mean(v * v, axis=-1, keepdims=True)
                o_ref[n] = v * lax.rsqrt(ms + EPS) * fw_ref[...]


def moe_experts(hn, pos, gate, w1, w3, w2, g2, x3, final_w, cap, nb, final_norm, name):
    b, tokens, d = hn.shape
    g2 = jnp.broadcast_to(g2, (b, 1, d))
    g2_spec = pl.BlockSpec((nb, 1, d), lambda i, e: (i, 0, 0))
    pos4 = pos.reshape(b, N_EXPERTS, 1, tokens)
    gate4 = gate.reshape(b, N_EXPERTS, 1, tokens)
    sel_spec = pl.BlockSpec((nb, None, 1, tokens), lambda i, e: (i, e, 0, 0))
    w_spec = pl.BlockSpec((None, d, d), lambda i, e: (e, 0, 0))
    tok_spec = pl.BlockSpec((nb, tokens, d), lambda i, e: (i, 0, 0), pipeline_mode=pl.Buffered(1))
    return pl.pallas_call(
        functools.partial(_expert_kernel, cap=cap, final_norm=final_norm),
        grid=(b // nb, N_EXPERTS),
        in_specs=[tok_spec, sel_spec, sel_spec, w_spec, w_spec, w_spec, g2_spec, tok_spec,
                  pl.BlockSpec((1, d), lambda i, e: (0, 0))],
        out_specs=tok_spec,
        out_shape=jax.ShapeDtypeStruct((b, tokens, d), F32),
        compiler_params=_cparams("parallel", "arbitrary"),
        name=name,
    )(hn, pos4, gate4, w1, w3, w2, g2, x3, final_w)


def moe_block(x3, wm, sh, g2, router_wt, w1, w3, w2, final_w, nb, final_norm, name):
    tokens = x3.shape[1]
    cap = EC_CAPACITY * tokens // N_EXPERTS
    hn, pos, gate = moe_router(x3, wm, sh, router_wt, cap, name + "_router")
    return moe_experts(hn, pos, gate, w1, w3, w2, g2, x3, final_w, cap, nb, final_norm, name + "_experts")


def kernel(x, c, ctx, c_ctx, ada_w, ada_b, norm1_w, norm2_w, ssd_in_w, ssd_conv_w, ssd_conv_b, ssd_dt_bias, ssd_A_log, ssd_D, ssd_norm_w, ssd_out_w, attn_qkv_w, attn_q_norm_w, attn_k_norm_w, attn_out_w, moe_router_w, moe_w1, moe_w3, moe_w2, final_norm_w):
    batch, seq, d = x.shape
    ctx_len = ctx.shape[1]
    depth = ada_w.shape[0]
    assert depth == 2 and d == D_MODEL

    mod_rows = -(-(batch + 1) // SUBLANES) * SUBLANES
    c_all = jnp.zeros((mod_rows, d), F32).at[:batch].set(c).at[batch].set(c_ctx)
    mod = ada_modulation(c_all, ada_w, ada_b)

    def mod_parts(layer):
        parts = jnp.split(mod[layer], 6, axis=-1)
        lat = [p[:batch].reshape(batch, 1, d) for p in parts]
        cx = [p[batch:batch + 1].reshape(1, 1, d) for p in parts]
        return lat, cx

    final_w = final_norm_w.reshape(1, d)
    x2 = x.reshape(batch * seq, d)
    c2 = ctx.reshape(batch * ctx_len, d)

    lat, cx = mod_parts(0)
    n1 = norm1_w[0].reshape(1, 1, d)
    n2 = norm2_w[0].reshape(1, 1, d)
    in_w = jnp.pad(ssd_in_w[0], ((0, 0), (0, SSD_IN_PAD - SSD_IN_DIM))).astype(BF16)
    proj_l = norm_mod_mm(x2, n1 * (1.0 + lat[1]), lat[0], in_w, seq, 1024, 896, "ssd_in_lat")
    proj_c = norm_mod_mm(c2, n1 * (1.0 + cx[1]), cx[0], in_w, batch * ctx_len, 1024, 896, "ssd_in_ctx")
    proj_l3 = proj_l.reshape(batch, seq, SSD_IN_PAD)
    proj_c3 = proj_c.reshape(batch, ctx_len, SSD_IN_PAD)
    dt_bias = jnp.pad(ssd_dt_bias[0].reshape(1, 2 * SSD_HEADS), ((0, 0), (0, LANES - 2 * SSD_HEADS)))
    a_neg = jnp.pad(-jnp.exp(ssd_A_log[0].astype(F32)).reshape(1, 2 * SSD_HEADS),
                    ((0, 0), (0, LANES - 2 * SSD_HEADS)))
    col_l, row_l = ssd_prep(proj_l3, dt_bias, a_neg, "ssd_prep_lat")
    col_c, row_c = ssd_prep(proj_c3, dt_bias, a_neg, "ssd_prep_ctx")
    dskip_e = jnp.repeat(ssd_D[0], SSD_HEADDIM).reshape(SSD_GROUPS, 1, SSD_HPG * SSD_HEADDIM)
    y_c, y_l = ssd_scan(proj_c3, proj_l3, ssd_conv_w[0], ssd_conv_b[0].reshape(1, CONV_DIM),
                        col_c, row_c, col_l, row_l, dskip_e)
    ssd_nw = ssd_norm_w[0].reshape(1, D_INNER)
    out_w = ssd_out_w[0].astype(BF16)
    x2 = ssd_out_mm(y_l.reshape(batch * seq, D_INNER), proj_l, ssd_nw, out_w, x2, lat[2], seq, 512,
                    "ssd_out_lat")
    c2 = ssd_out_mm(y_c.reshape(batch * ctx_len, D_INNER), proj_c, ssd_nw, out_w, c2, cx[2],
                    batch * ctx_len, 512, "ssd_out_ctx")

    rwt = moe_router_w[0].T
    w1, w3, w2 = moe_w1[0].astype(BF16), moe_w3[0].astype(BF16), moe_w2[0].astype(BF16)
    x3 = moe_block(x2.reshape(batch, seq, d), n2 * (1.0 + lat[4]), lat[3], lat[5], rwt, w1, w3, w2,
                   final_w, 1, False, "moe0_lat")
    c3 = moe_block(c2.reshape(batch, ctx_len, d), n2 * (1.0 + cx[4]), cx[3], cx[5], rwt, w1, w3, w2,
                   final_w, 8, False, "moe0_ctx")
    x2 = x3.reshape(batch * seq, d)
    c2 = c3.reshape(batch * ctx_len, d)

    lat, cx = mod_parts(1)
    n1 = norm1_w[1].reshape(1, 1, d)
    n2 = norm2_w[1].reshape(1, 1, d)
    qkv_w = attn_qkv_w[0].astype(BF16)
    qkv_l = norm_mod_mm(x2, n1 * (1.0 + lat[1]), lat[0], qkv_w, seq, 1024, 768, "qkv_lat")
    qkv_c = norm_mod_mm(c2, n1 * (1.0 + cx[1]), cx[0], qkv_w, batch * ctx_len, 1024, 768, "qkv_ctx")
    tables = _attn_tables(seq)
    q_l, k_l, v_l = qk_prep(qkv_l, attn_q_norm_w[0], attn_k_norm_w[0], tables, seq, True, 512, "qk_prep_lat")
    _, k_c, v_c = qk_prep(qkv_c, attn_q_norm_w[0], attn_k_norm_w[0], tables, ctx_len, False, ctx_len,
                          "qk_prep_ctx")
    o = attention(q_l, k_l, v_l, k_c, v_c, batch, seq, ctx_len, 128)
    x2 = res_mm(o, attn_out_w[0].astype(BF16), x2, lat[2], seq, 1024, "attn_out")

    rwt = moe_router_w[1].T
    w1, w3, w2 = moe_w1[1].astype(BF16), moe_w3[1].astype(BF16), moe_w2[1].astype(BF16)
    x3 = moe_block(x2.reshape(batch, seq, d), n2 * (1.0 + lat[4]), lat[3], lat[5], rwt, w1, w3, w2,
                   final_w, 1, True, "moe1_lat")
    return x3
```

```python
import functools
import math

import numpy as np
import jax
import jax.numpy as jnp
from jax import lax
from jax.experimental import pallas as pl
from jax.experimental.pallas import tpu as pltpu

F32 = jnp.float32
BF16 = jnp.bfloat16
HIGHEST = lax.Precision.HIGHEST

D_MODEL = 1024
GRID_W = 64
EPS = 1e-6
D_INNER = 2048
SSD_HEADDIM = 64
SSD_HEADS = 32
SSD_GROUPS = 8
SSD_HPG = 4
D_STATE = 128
CONV_W = 5
SSD_CHUNK = 128
CONV_DIM = D_INNER + 2 * SSD_GROUPS * D_STATE
SSD_IN_DIM = 2 * D_INNER + 2 * SSD_GROUPS * D_STATE + 2 * SSD_HEADS
SSD_IN_PAD = 6272
HEAD_DIM = 64
N_Q_HEADS = 16
N_KV_HEADS = 4
Q_PER_KV = 4
D_ATTN = 1024
KV_DIM = N_KV_HEADS * HEAD_DIM
QKV_DIM = D_ATTN + 2 * KV_DIM
ROPE_THETA = 10000.0
N_EXPERTS = 16
EC_CAPACITY = 2

LANES = 128
SUBLANES = 8
VMEM_LIMIT_BYTES = 56 * 1024 * 1024


def _cparams(*sem):
    return pltpu.CompilerParams(dimension_semantics=sem, vmem_limit_bytes=VMEM_LIMIT_BYTES)


def _sigmoid(x):
    return 1.0 / (1.0 + jnp.exp(-x))


def _silu(x):
    return x * _sigmoid(x)


def _ada_kernel(c_ref, w_ref, b_ref, o_ref):
    a = _silu(c_ref[...])
    o_ref[...] = jnp.dot(a, w_ref[...], precision=HIGHEST, preferred_element_type=F32) + b_ref[...]


def ada_modulation(c_all, ada_w, ada_b):
    depth, d, n = ada_w.shape
    m = c_all.shape[0]
    tn = 1536
    return pl.pallas_call(
        _ada_kernel,
        grid=(depth, n // tn),
        in_specs=[pl.BlockSpec((m, d), lambda l, j: (0, 0)),
                  pl.BlockSpec((None, d, tn), lambda l, j: (l, 0, j)),
                  pl.BlockSpec((None, 1, tn), lambda l, j: (l, 0, j))],
        out_specs=pl.BlockSpec((None, m, tn), lambda l, j: (l, 0, j)),
        out_shape=jax.ShapeDtypeStruct((depth, m, n), F32),
        compiler_params=_cparams("parallel", "parallel"),
        name="ada_modulation",
    )(c_all, ada_w, ada_b.reshape(depth, 1, n))


def _norm_mod_mm_kernel(x_ref, wm_ref, sh_ref, w_ref, o_ref, a_scr):
    @pl.when(pl.program_id(1) == 0)
    def _():
        x = x_ref[...]
        ms = jnp.mean(x * x, axis=-1, keepdims=True)
        a_scr[...] = (x * lax.rsqrt(ms + EPS) * wm_ref[...] + sh_ref[...]).astype(BF16)

    o_ref[...] = jnp.dot(a_scr[...], w_ref[...], preferred_element_type=F32).astype(o_ref.dtype)


def norm_mod_mm(x2d, wm, sh, w, rows_per_group, tm, tn, name):
    m, k = x2d.shape
    n = w.shape[1]
    tiles_per_group = rows_per_group // tm
    return pl.pallas_call(
        _norm_mod_mm_kernel,
        grid=(m // tm, n // tn),
        in_specs=[pl.BlockSpec((tm, k), lambda i, j: (i, 0)),
                  pl.BlockSpec((None, 1, k), lambda i, j: (i // tiles_per_group, 0, 0)),
                  pl.BlockSpec((None, 1, k), lambda i, j: (i // tiles_per_group, 0, 0)),
                  pl.BlockSpec((k, tn), lambda i, j: (0, j))],
        out_specs=pl.BlockSpec((tm, tn), lambda i, j: (i, j)),
        out_shape=jax.ShapeDtypeStruct((m, n), F32),
        scratch_shapes=[pltpu.VMEM((tm, k), BF16)],
        compiler_params=_cparams("parallel", "arbitrary"),
        name=name,
    )(x2d, wm, sh, w)


def _res_mm_kernel(a_ref, w_ref, res_ref, gate_ref, o_ref):
    acc = jnp.dot(a_ref[...], w_ref[...], preferred_element_type=F32)
    o_ref[...] = res_ref[...] + gate_ref[...] * acc


def res_mm(a, w, res, gate, rows_per_group, tm, name):
    m, k = a.shape
    n = w.shape[1]
    tiles_per_group = rows_per_group // tm
    return pl.pallas_call(
        _res_mm_kernel,
        grid=(m // tm,),
        in_specs=[pl.BlockSpec((tm, k), lambda i: (i, 0)),
                  pl.BlockSpec((k, n), lambda i: (0, 0)),
                  pl.BlockSpec((tm, n), lambda i: (i, 0)),
                  pl.BlockSpec((None, 1, n), lambda i: (i // tiles_per_group, 0, 0))],
        out_specs=pl.BlockSpec((tm, n), lambda i: (i, 0)),
        out_shape=jax.ShapeDtypeStruct((m, n), F32),
        compiler_params=_cparams("parallel"),
        name=name,
    )(a, w, res, gate)


def _ssd_out_kernel(y_ref, z_ref, nw_ref, w_ref, res_ref, gate_ref, o_ref):
    v = y_ref[...] * _silu(z_ref[...])
    ms = jnp.mean(v * v, axis=-1, keepdims=True)
    a = (v * lax.rsqrt(ms + EPS) * nw_ref[...]).astype(BF16)
    acc = jnp.dot(a, w_ref[...], preferred_element_type=F32)
    o_ref[...] = res_ref[...] + gate_ref[...] * acc


def ssd_out_mm(y, proj, norm_w, w, res, gate, rows_per_group, tm, name):
    m, k = y.shape
    n = w.shape[1]
    tiles_per_group = rows_per_group // tm
    return pl.pallas_call(
        _ssd_out_kernel,
        grid=(m // tm,),
        in_specs=[pl.BlockSpec((tm, k), lambda i: (i, 0)),
                  pl.BlockSpec((tm, k), lambda i: (i, 0)),
                  pl.BlockSpec((1, k), lambda i: (0, 0)),
                  pl.BlockSpec((k, n), lambda i: (0, 0)),
                  pl.BlockSpec((tm, n), lambda i: (i, 0)),
                  pl.BlockSpec((None, 1, n), lambda i: (i // tiles_per_group, 0, 0))],
        out_specs=pl.BlockSpec((tm, n), lambda i: (i, 0)),
        out_shape=jax.ShapeDtypeStruct((m, n), F32),
        compiler_params=_cparams("parallel"),
        name=name,
    )(y, proj, norm_w, w, res, gate)


def _ssd_prep_kernel(dtr_ref, bias_ref, a_ref, scol_ref, srow_ref, col_ref, row_ref):
    seq = dtr_ref.shape[0]
    x = dtr_ref[...] + bias_ref[...]
    dt = jnp.maximum(x, 0.0) + jnp.log1p(jnp.exp(-jnp.abs(x)))
    a = dt * a_ref[...]
    t_i = lax.broadcasted_iota(jnp.int32, (SSD_CHUNK, SSD_CHUNK), 0)
    s_i = lax.broadcasted_iota(jnp.int32, (SSD_CHUNK, SSD_CHUNK), 1)
    tri_lo = (s_i <= t_i).astype(F32)
    tri_up = (s_i >= t_i).astype(F32)
    fwd_lane = lax.broadcasted_iota(jnp.int32, (SSD_CHUNK, LANES), 1) < SSD_HEADS
    chunks = []
    for c in range(seq // SSD_CHUNK):
        ac = a[c * SSD_CHUNK:(c + 1) * SSD_CHUNK]
        cf = jnp.dot(tri_lo, ac, precision=HIGHEST, preferred_element_type=F32)
        cb = jnp.dot(tri_up, ac, precision=HIGHEST, preferred_element_type=F32)
        chunks.append(jnp.where(fwd_lane, cf, cb))
    cum = jnp.concatenate(chunks, axis=0)
    both = jnp.concatenate([dt, cum], axis=1)
    col_ref[...] = jnp.dot(both, scol_ref[...], precision=HIGHEST, preferred_element_type=F32)
    row_ref[...] = lax.dot_general(srow_ref[...], cum, (((1,), (1,)), ((), ())),
                                   precision=HIGHEST, preferred_element_type=F32)


def _ssd_select_tables():
    scol = np.zeros((2 * LANES, SSD_GROUPS * LANES), np.float32)
    srow = np.zeros((SSD_GROUPS * SUBLANES, LANES), np.float32)
    for g in range(SSD_GROUPS):
        for d in range(2):
            for r in range(SSD_HPG):
                lane = d * SSD_HEADS + SSD_HPG * g + r
                scol[lane, g * LANES + d * SSD_HPG + r] = 1.0
                scol[LANES + lane, g * LANES + 2 * SSD_HPG + d * SSD_HPG + r] = 1.0
                srow[g * SUBLANES + d * SSD_HPG + r, lane] = 1.0
    return jnp.asarray(scol), jnp.asarray(srow)


def ssd_prep(proj3, dt_bias, a_neg, name):
    b, seq, _ = proj3.shape
    scol, srow = _ssd_select_tables()
    dt_blk = (2 * D_INNER + 2 * SSD_GROUPS * D_STATE) // LANES
    return pl.pallas_call(
        _ssd_prep_kernel,
        grid=(b,),
        in_specs=[pl.BlockSpec((None, seq, LANES), lambda i: (i, 0, dt_blk)),
                  pl.BlockSpec((1, LANES), lambda i: (0, 0)),
                  pl.BlockSpec((1, LANES), lambda i: (0, 0)),
                  pl.BlockSpec(scol.shape, lambda i: (0, 0)),
                  pl.BlockSpec(srow.shape, lambda i: (0, 0))],
        out_specs=[pl.BlockSpec((None, seq, SSD_GROUPS * LANES), lambda i: (i, 0, 0)),
                   pl.BlockSpec((None, SSD_GROUPS * SUBLANES, seq), lambda i: (i, 0, 0))],
        out_shape=[jax.ShapeDtypeStruct((b, seq, SSD_GROUPS * LANES), F32),
                   jax.ShapeDtypeStruct((b, SSD_GROUPS * SUBLANES, seq), F32)],
        compiler_params=_cparams("parallel"),
        name=name,
    )(proj3, dt_bias, a_neg, scol, srow)


CONV_HALO = SUBLANES


def _ssd_scan_kernel(xc_ref, bc_ref, cc_ref, xl_ref, bl_ref, cl_ref,
                     cwx_ref, cwb_ref, cwc_ref, cbx_ref, cbb_ref, cbc_ref,
                     colc_ref, rowc_ref, coll_ref, rowl_ref, dsk_ref,
                     yc_ref, yl_ref,
                     pad_scr, xsc, bsc, csc, xsl, bsl, csl, hf_scr, hb_scr):
    q = SSD_CHUNK
    width = SSD_HPG * SSD_HEADDIM
    dsk = dsk_ref[...]

    def conv_silu(raw_ref, w_ref, b_ref, store):
        seq, ch = raw_ref.shape
        zeros = jnp.zeros((CONV_HALO, ch), F32)
        pad_scr[0:CONV_HALO, 0:ch] = zeros
        pad_scr[CONV_HALO + seq:2 * CONV_HALO + seq, 0:ch] = zeros

        def copy_body(c, carry):
            base = pl.multiple_of(c * q, q)
            dst = pl.multiple_of(base + CONV_HALO, SUBLANES)
            pad_scr[pl.ds(dst, q), 0:ch] = raw_ref[pl.ds(base, q), :]
            return carry

        lax.fori_loop(0, seq // q, copy_body, 0)
        w = w_ref[...]
        bias = b_ref[...]

        def conv_body(c, carry):
            base = pl.multiple_of(c * q, q)
            win = pad_scr[pl.ds(base, q + 2 * CONV_HALO), 0:ch]
            acc = bias
            for k in range(CONV_W):
                off = CONV_HALO - CONV_W // 2 + k
                acc = acc + win[off:off + q] * w[k:k + 1]
            store(base, _silu(acc))
            return carry

        lax.fori_loop(0, seq // q, conv_body, 0)

    def x_store(xs_ref, y_ref):
        def store(base, v):
            xs_ref[pl.ds(base, q), :] = v
            y_ref[pl.ds(base, q), :] = dsk * v
        return store

    def b_store(bt_ref):
        def store(base, v):
            bt_ref[:, pl.ds(base, q)] = v.T.astype(BF16)
        return store

    def c_store(c_ref):
        def store(base, v):
            c_ref[pl.ds(base, q), :] = v.astype(BF16)
        return store

    conv_silu(xc_ref, cwx_ref, cbx_ref, x_store(xsc, yc_ref))
    conv_silu(bc_ref, cwb_ref, cbb_ref, b_store(bsc))
    conv_silu(cc_ref, cwc_ref, cbc_ref, c_store(csc))
    conv_silu(xl_ref, cwx_ref, cbx_ref, x_store(xsl, yl_ref))
    conv_silu(bl_ref, cwb_ref, cbb_ref, b_store(bsl))
    conv_silu(cl_ref, cwc_ref, cbc_ref, c_store(csl))

    t_i = lax.broadcasted_iota(jnp.int32, (q, q), 0)
    s_i = lax.broadcasted_iota(jnp.int32, (q, q), 1)
    head_of_lane = lax.broadcasted_iota(jnp.int32, (q, width), 1) // SSD_HEADDIM

    def per_head(parts):
        out = parts[SSD_HPG - 1]
        for r in range(SSD_HPG - 2, -1, -1):
            out = jnp.where(head_of_lane == r, parts[r], out)
        return out

    def run_seq(xs_ref, bt_ref, c_ref, col_ref, row_ref, y_ref):
        nchunks = xs_ref.shape[0] // q

        def chunk(c, d, h_scr):
            mask = (s_i <= t_i) if d == 0 else (s_i >= t_i)
            edge = q - 1 if d == 0 else 0
            base = pl.multiple_of(c * q, q)
            xs = xs_ref[pl.ds(base, q), :]
            bq_t = bt_ref[:, pl.ds(base, q)]
            cq = c_ref[pl.ds(base, q), :]
            cp = col_ref[pl.ds(base, q), :]
            rp = row_ref[:, pl.ds(base, q)]
            cb = jnp.dot(cq, bq_t, preferred_element_type=F32)
            dt_cols = [cp[:, SSD_HPG * d + r:SSD_HPG * d + r + 1] for r in range(SSD_HPG)]
            cum_cols = [cp[:, 2 * SSD_HPG + SSD_HPG * d + r:2 * SSD_HPG + SSD_HPG * d + r + 1]
                        for r in range(SSD_HPG)]
            dt_e = per_head(dt_cols)
            cum_e = per_head(cum_cols)
            xdt = xs * dt_e
            ms = []
            for r in range(SSD_HPG):
                row = rp[SSD_HPG * d + r:SSD_HPG * d + r + 1, :]
                seg = jnp.exp(jnp.where(mask, cum_cols[r] - row, -1e30))
                ms.append((cb * seg).astype(BF16))
            stacked = jnp.dot(jnp.concatenate(ms, axis=0), xdt.astype(BF16),
                              preferred_element_type=F32)
            y = per_head([stacked[r * q:(r + 1) * q] for r in range(SSD_HPG)])
            h_t = h_scr[...]
            y = y + jnp.dot(cq, h_t.astype(BF16), preferred_element_type=F32) * jnp.exp(cum_e)
            cum_edge = cum_e[edge:edge + 1, :]
            decay_to_edge = jnp.exp(cum_edge - cum_e)
            h_scr[...] = h_t * jnp.exp(cum_edge) + jnp.dot(
                bq_t, (xdt * decay_to_edge).astype(BF16), preferred_element_type=F32)
            y_ref[pl.ds(base, q), :] += y

        def body(i, carry):
            chunk(i, 0, hf_scr)
            chunk(nchunks - 1 - i, 1, hb_scr)
            return carry

        lax.fori_loop(0, nchunks, body, 0)

    hf_scr[...] = jnp.zeros(hf_scr.shape, F32)
    hb_scr[...] = jnp.zeros(hb_scr.shape, F32)
    run_seq(xsc, bsc, csc, colc_ref, rowc_ref, yc_ref)
    run_seq(xsl, bsl, csl, coll_ref, rowl_ref, yl_ref)


def ssd_scan(proj_c, proj_l, conv_w, conv_b, col_c, row_c, col_l, row_l, dskip_e):
    b, lc, _ = proj_c.shape
    ll = proj_l.shape[1]
    width = SSD_HPG * SSD_HEADDIM
    x0 = D_INNER // width
    b0 = (2 * D_INNER) // D_STATE
    c0 = b0 + SSD_GROUPS
    wx0, wb0, wc0 = 0, D_INNER // D_STATE, D_INNER // D_STATE + SSD_GROUPS

    def seq_specs(seq):
        return [pl.BlockSpec((None, seq, width), lambda i, g: (i, 0, x0 + g)),
                pl.BlockSpec((None, seq, D_STATE), lambda i, g: (i, 0, b0 + g)),
                pl.BlockSpec((None, seq, D_STATE), lambda i, g: (i, 0, c0 + g))]

    def par_specs(rows):
        return [pl.BlockSpec((rows, width), lambda i, g: (0, wx0 + g)),
                pl.BlockSpec((rows, D_STATE), lambda i, g: (0, wb0 + g)),
                pl.BlockSpec((rows, D_STATE), lambda i, g: (0, wc0 + g))]

    def pack_specs(seq):
        return [pl.BlockSpec((None, seq, LANES), lambda i, g: (i, 0, g)),
                pl.BlockSpec((None, SUBLANES, seq), lambda i, g: (i, g, 0))]

    return pl.pallas_call(
        _ssd_scan_kernel,
        grid=(b, SSD_GROUPS),
        in_specs=(seq_specs(lc) + seq_specs(ll) + par_specs(CONV_W) + par_specs(1)
                  + pack_specs(lc) + pack_specs(ll)
                  + [pl.BlockSpec((None, 1, width), lambda i, g: (g, 0, 0))]),
        out_specs=[pl.BlockSpec((None, lc, width), lambda i, g: (i, 0, g)),
                   pl.BlockSpec((None, ll, width), lambda i, g: (i, 0, g))],
        out_shape=[jax.ShapeDtypeStruct((b, lc, D_INNER), F32),
                   jax.ShapeDtypeStruct((b, ll, D_INNER), F32)],
        scratch_shapes=[pltpu.VMEM((ll + 2 * CONV_HALO, width), F32),
                        pltpu.VMEM((lc, width), F32), pltpu.VMEM((D_STATE, lc), BF16),
                        pltpu.VMEM((lc, D_STATE), BF16),
                        pltpu.VMEM((ll, width), F32), pltpu.VMEM((D_STATE, ll), BF16),
                        pltpu.VMEM((ll, D_STATE), BF16),
                        pltpu.VMEM((D_STATE, width), F32), pltpu.VMEM((D_STATE, width), F32)],
        compiler_params=_cparams("parallel", "parallel"),
        name="ssd_scan",
    )(proj_c, proj_c, proj_c, proj_l, proj_l, proj_l,
      conv_w, conv_w, conv_w, conv_b, conv_b, conv_b,
      col_c, row_c, col_l, row_l, dskip_e)


def _qk_prep_kernel(qkv_ref, gmat_ref, qw_ref, kw_ref, rope_c_ref, rope_a_ref, rope_b_ref, rep_ref,
                    q_ref, k_ref, v_ref, *, use_rope):
    blk = Q_PER_KV * HEAD_DIM
    gmat = gmat_ref[...]

    def norm_rope(x, w):
        ms = jnp.dot(x * x, gmat, precision=HIGHEST, preferred_element_type=F32)
        xn = x * lax.rsqrt(ms + EPS) * w
        if use_rope:
            half = HEAD_DIM // 2
            xn = (xn * rope_c_ref[...] + pltpu.roll(xn, blk - half, 1) * rope_a_ref[...]
                  + pltpu.roll(xn, half, 1) * rope_b_ref[...])
        return xn

    scale = 1.0 / math.sqrt(HEAD_DIM)
    for j in range(D_ATTN // blk):
        xq = norm_rope(qkv_ref[:, j * blk:(j + 1) * blk], qw_ref[...])
        q_ref[:, j * blk:(j + 1) * blk] = (xq * scale).astype(BF16)
    xk = norm_rope(qkv_ref[:, D_ATTN:D_ATTN + KV_DIM], kw_ref[...]).astype(BF16)
    xv = qkv_ref[:, D_ATTN + KV_DIM:D_ATTN + 2 * KV_DIM].astype(BF16)
    for kv in range(N_KV_HEADS):
        rep = rep_ref[kv]
        k_ref[:, kv * blk:(kv + 1) * blk] = jnp.dot(xk, rep, preferred_element_type=F32).astype(BF16)
        v_ref[:, kv * blk:(kv + 1) * blk] = jnp.dot(xv, rep, preferred_element_type=F32).astype(BF16)


def _attn_tables(seq):
    rows = seq // GRID_W
    row = np.repeat(np.arange(rows), GRID_W).astype(np.float32)
    col = np.tile(np.arange(GRID_W), rows).astype(np.float32)
    axis_dim = HEAD_DIM // 2
    inv_freq = jnp.asarray(ROPE_THETA, F32) ** (-jnp.arange(0, axis_dim, 2, dtype=F32) / axis_dim)
    ang = jnp.concatenate([jnp.asarray(row)[:, None] * inv_freq, jnp.asarray(col)[:, None] * inv_freq], axis=-1)
    cos, sin = jnp.cos(ang), jnp.sin(ang)
    zero = jnp.zeros_like(sin)
    reps = Q_PER_KV
    rope_c = jnp.tile(jnp.concatenate([cos, cos], axis=-1), (1, reps))
    rope_a = jnp.tile(jnp.concatenate([-sin, zero], axis=-1), (1, reps))
    rope_b = jnp.tile(jnp.concatenate([zero, sin], axis=-1), (1, reps))
    return rope_c, rope_a, rope_b


def _attn_consts():
    blk = Q_PER_KV * HEAD_DIM
    head = np.arange(blk) // HEAD_DIM
    gmat = (head[:, None] == head[None, :]).astype(np.float32) / HEAD_DIM
    rep = np.zeros((N_KV_HEADS, KV_DIM, blk), np.float32)
    for kv in range(N_KV_HEADS):
        for j in range(blk):
            rep[kv, kv * HEAD_DIM + j % HEAD_DIM, j] = 1.0
    return jnp.asarray(gmat), jnp.asarray(rep, dtype=BF16)


def qk_prep(qkv, q_norm_w, k_norm_w, tables, seq, use_rope, tl, name):
    m = qkv.shape[0]
    blk = Q_PER_KV * HEAD_DIM
    gmat, rep = _attn_consts()
    qw = jnp.tile(q_norm_w.reshape(1, HEAD_DIM), (1, Q_PER_KV))
    kw = jnp.tile(k_norm_w.reshape(1, HEAD_DIM), (1, N_KV_HEADS))
    tiles_per_seq = seq // tl
    tab_spec = pl.BlockSpec((tl, blk), lambda i: (i % tiles_per_seq, 0))
    out_w = N_KV_HEADS * blk
    return pl.pallas_call(
        functools.partial(_qk_prep_kernel, use_rope=use_rope),
        grid=(m // tl,),
        in_specs=[pl.BlockSpec((tl, QKV_DIM), lambda i: (i, 0)),
                  pl.BlockSpec((blk, blk), lambda i: (0, 0)),
                  pl.BlockSpec((1, blk), lambda i: (0, 0)),
                  pl.BlockSpec((1, blk), lambda i: (0, 0)),
                  tab_spec, tab_spec, tab_spec,
                  pl.BlockSpec(rep.shape, lambda i: (0, 0, 0))],
        out_specs=[pl.BlockSpec((tl, D_ATTN), lambda i: (i, 0)),
                   pl.BlockSpec((tl, out_w), lambda i: (i, 0)),
                   pl.BlockSpec((tl, out_w), lambda i: (i, 0))],
        out_shape=[jax.ShapeDtypeStruct((m, D_ATTN), BF16),
                   jax.ShapeDtypeStruct((m, out_w), BF16),
                   jax.ShapeDtypeStruct((m, out_w), BF16)],
        compiler_params=_cparams("parallel"),
        name=name,
    )(qkv, gmat, qw, kw, *tables, rep)


def _attn_kernel(q_ref, kl_ref, vl_ref, kc_ref, vc_ref, o_ref):
    tq = q_ref.shape[0]
    blk = Q_PER_KV * HEAD_DIM
    head_of_lane = lax.broadcasted_iota(jnp.int32, (tq, blk), 1) // HEAD_DIM
    nt = (((1,), (1,)), ((), ()))
    for kv in range(N_KV_HEADS):
        cols = slice(kv * blk, (kv + 1) * blk)
        q4 = q_ref[:, cols]
        zero = jnp.zeros_like(q4)
        qs = jnp.concatenate([jnp.where(head_of_lane == g, q4, zero) for g in range(Q_PER_KV)], axis=0)
        s_l = lax.dot_general(qs, kl_ref[:, cols], nt, preferred_element_type=F32)
        s_c = lax.dot_general(qs, kc_ref[:, cols], nt, preferred_element_type=F32)
        m = jnp.maximum(jnp.max(s_l, axis=-1, keepdims=True), jnp.max(s_c, axis=-1, keepdims=True))
        p_l = jnp.exp(s_l - m)
        p_c = jnp.exp(s_c - m)
        denom = jnp.sum(p_l, axis=-1, keepdims=True) + jnp.sum(p_c, axis=-1, keepdims=True)
        o = (jnp.dot(p_l.astype(BF16), vl_ref[:, cols], preferred_element_type=F32)
             + jnp.dot(p_c.astype(BF16), vc_ref[:, cols], preferred_element_type=F32)) / denom
        out = o[(Q_PER_KV - 1) * tq:Q_PER_KV * tq]
        for g in range(Q_PER_KV - 2, -1, -1):
            out = jnp.where(head_of_lane == g, o[g * tq:(g + 1) * tq], out)
        o_ref[:, cols] = out.astype(BF16)


def attention(q, k_lat, v_lat, k_ctx, v_ctx, batch, seq, ctx_len, tq):
    width = q.shape[1]
    nq = seq // tq
    return pl.pallas_call(
        _attn_kernel,
        grid=(batch, nq),
        in_specs=[pl.BlockSpec((tq, width), lambda b, i: (b * nq + i, 0)),
                  pl.BlockSpec((seq, width), lambda b, i: (b, 0)),
                  pl.BlockSpec((seq, width), lambda b, i: (b, 0)),
                  pl.BlockSpec((ctx_len, width), lambda b, i: (b, 0)),
                  pl.BlockSpec((ctx_len, width), lambda b, i: (b, 0))],
        out_specs=pl.BlockSpec((tq, width), lambda b, i: (b * nq + i, 0)),
        out_shape=jax.ShapeDtypeStruct((batch * seq, width), BF16),
        compiler_params=_cparams("parallel", "parallel"),
        name="attention",
    )(q, k_lat, v_lat, k_ctx, v_ctx)


def _router_kernel(x_ref, wm_ref, sh_ref, rw_ref, hn_ref, pos_ref, gate_ref, *, cap):
    tokens = x_ref.shape[0]
    x = x_ref[...]
    ms = jnp.mean(x * x, axis=-1, keepdims=True)
    hn = x * lax.rsqrt(ms + EPS) * wm_ref[...] + sh_ref[...]
    hn_ref[...] = hn.astype(BF16)
    logits = lax.dot_general(rw_ref[...], hn, (((1,), (1,)), ((), ())),
                             precision=HIGHEST, preferred_element_type=F32)
    e = jnp.exp(logits - jnp.max(logits, axis=0, keepdims=True))
    aff = e / jnp.sum(e, axis=0, keepdims=True)
    gate_ref[...] = aff
    bits = pltpu.bitcast(aff, jnp.int32)

    def count(flags):
        return jnp.sum(jnp.where(flags, 1.0, 0.0), axis=1, keepdims=True)

    def search(i, thr):
        cand = thr | jnp.left_shift(jnp.int32(1), 30 - i)
        return jnp.where(count(bits >= cand) >= cap, cand, thr)

    thr = lax.fori_loop(0, 31, search, jnp.zeros((N_EXPERTS, 1), jnp.int32))
    above = bits > thr
    tied = bits == thr
    need = cap - count(above)

    t_i = lax.broadcasted_iota(jnp.int32, (LANES, LANES), 0)
    s_i = lax.broadcasted_iota(jnp.int32, (LANES, LANES), 1)
    tri = (t_i <= s_i).astype(BF16)

    def lane_cumsum(flags):
        carry = jnp.zeros((N_EXPERTS, 1), F32)
        parts = []
        for c in range(tokens // LANES):
            part = jnp.dot(flags[:, c * LANES:(c + 1) * LANES].astype(BF16), tri,
                           preferred_element_type=F32) + carry
            parts.append(part)
            carry = part[:, LANES - 1:LANES]
        return jnp.concatenate(parts, axis=1)

    tie_rank = lane_cumsum(jnp.where(tied, 1.0, 0.0))
    sel = above | (tied & (tie_rank <= need))
    slot = lane_cumsum(jnp.where(sel, 1.0, 0.0)) - 1.0
    pos_ref[...] = jnp.where(sel, slot, -1.0).astype(jnp.int32)


def moe_router(x3, wm, sh, router_wt, cap, name):
    b, tokens, d = x3.shape
    per_batch = wm.shape[0] > 1
    mod_spec = pl.BlockSpec((None, 1, d), (lambda i: (i, 0, 0)) if per_batch else (lambda i: (0, 0, 0)))
    return pl.pallas_call(
        functools.partial(_router_kernel, cap=cap),
        grid=(b,),
        in_specs=[pl.BlockSpec((None, tokens, d), lambda i: (i, 0, 0)),
                  mod_spec, mod_spec,
                  pl.BlockSpec((N_EXPERTS, d), lambda i: (0, 0))],
        out_specs=[pl.BlockSpec((None, tokens, d), lambda i: (i, 0, 0)),
                   pl.BlockSpec((None, N_EXPERTS, tokens), lambda i: (i, 0, 0)),
                   pl.BlockSpec((None, N_EXPERTS, tokens), lambda i: (i, 0, 0))],
        out_shape=[jax.ShapeDtypeStruct((b, tokens, d), BF16),
                   jax.ShapeDtypeStruct((b, N_EXPERTS, tokens), jnp.int32),
                   jax.ShapeDtypeStruct((b, N_EXPERTS, tokens), F32)],
        compiler_params=_cparams("parallel"),
        name=name,
    )(x3, wm, sh, router_wt)


def _expert_kernel(hn_ref, pos_ref, gate_ref, w1_ref, w3_ref, w2_ref, g2_ref, x_ref, fw_ref, o_ref,
                   *, cap, final_norm):
    nb, tokens, _ = hn_ref.shape
    e = pl.program_id(1)

    @pl.when(e == 0)
    def _():
        o_ref[...] = x_ref[...]

    slot_i = lax.broadcasted_iota(jnp.int32, (cap, tokens), 0)
    onehots, gathered, gates = [], [], []
    for n in range(nb):
        hit = slot_i == pos_ref[n]
        onehot = jnp.where(hit, 1.0, 0.0).astype(BF16)
        onehots.append(onehot)
        gates.append(jnp.sum(jnp.where(hit, gate_ref[n], 0.0), axis=1, keepdims=True))
        gathered.append(jnp.dot(onehot, hn_ref[n], preferred_element_type=F32).astype(BF16))
    xg = jnp.concatenate(gathered, axis=0) if nb > 1 else gathered[0]
    a = jnp.dot(xg, w1_ref[...], preferred_element_type=F32)
    g = jnp.dot(xg, w3_ref[...], preferred_element_type=F32)
    y = jnp.dot((_silu(a) * g).astype(BF16), w2_ref[...], preferred_element_type=F32)
    for n in range(nb):
        yn = (y[n * cap:(n + 1) * cap] * gates[n] * g2_ref[n]).astype(BF16)
        o_ref[n] += lax.dot_general(onehots[n], yn, (((0,), (0,)), ((), ())),
                                    preferred_element_type=F32)

    if final_norm:
        @pl.when(e == pl.num_programs(1) - 1)
        def _():
            for n in range(nb):
                v = o_ref[n]
                ms = jnp.mean(v * v, axis=-1, keepdims=True)
                o_ref[n] = v * lax.rsqrt(ms + EPS) * fw_ref[...]


def moe_experts(hn, pos, gate, w1, w3, w2, g2, x3, final_w, cap, nb, final_norm, name):
    b, tokens, d = hn.shape
    g2 = jnp.broadcast_to(g2, (b, 1, d))
    g2_spec = pl.BlockSpec((nb, 1, d), lambda i, e: (i, 0, 0))
    pos4 = pos.reshape(b, N_EXPERTS, 1, tokens)
    gate4 = gate.reshape(b, N_EXPERTS, 1, tokens)
    sel_spec = pl.BlockSpec((nb, None, 1, tokens), lambda i, e: (i, e, 0, 0))
    w_spec = pl.BlockSpec((None, d, d), lambda i, e: (e, 0, 0))
    tok_spec = pl.BlockSpec((nb, tokens, d), lambda i, e: (i, 0, 0), pipeline_mode=pl.Buffered(1))
    return pl.pallas_call(
        functools.partial(_expert_kernel, cap=cap, final_norm=final_norm),
        grid=(b // nb, N_EXPERTS),
        in_specs=[tok_spec, sel_spec, sel_spec, w_spec, w_spec, w_spec, g2_spec, tok_spec,
                  pl.BlockSpec((1, d), lambda i, e: (0, 0))],
        out_specs=tok_spec,
        out_shape=jax.ShapeDtypeStruct((b, tokens, d), F32),
        compiler_params=_cparams("parallel", "arbitrary"),
        name=name,
    )(hn, pos4, gate4, w1, w3, w2, g2, x3, final_w)


def moe_block(x3, wm, sh, g2, router_wt, w1, w3, w2, final_w, nb, final_norm, name):
    tokens = x3.shape[1]
    cap = EC_CAPACITY * tokens // N_EXPERTS
    hn, pos, gate = moe_router(x3, wm, sh, router_wt, cap, name + "_router")
    return moe_experts(hn, pos, gate, w1, w3, w2, g2, x3, final_w, cap, nb, final_norm, name + "_experts")


def kernel(x, c, ctx, c_ctx, ada_w, ada_b, norm1_w, norm2_w, ssd_in_w, ssd_conv_w, ssd_conv_b, ssd_dt_bias, ssd_A_log, ssd_D, ssd_norm_w, ssd_out_w, attn_qkv_w, attn_q_norm_w, attn_k_norm_w, attn_out_w, moe_router_w, moe_w1, moe_w3, moe_w2, final_norm_w):
    batch, seq, d = x.shape
    ctx_len = ctx.shape[1]
    depth = ada_w.shape[0]
    assert depth == 2 and d == D_MODEL

    mod_rows = -(-(batch + 1) // SUBLANES) * SUBLANES
    c_all = jnp.zeros((mod_rows, d), F32).at[:batch].set(c).at[batch].set(c_ctx)
    mod = ada_modulation(c_all, ada_w, ada_b)

    def mod_parts(layer):
        parts = jnp.split(mod[layer], 6, axis=-1)
        lat = [p[:batch].reshape(batch, 1, d) for p in parts]
        cx = [p[batch:batch + 1].reshape(1, 1, d) for p in parts]
        return lat, cx

    final_w = final_norm_w.reshape(1, d)
    x2 = x.reshape(batch * seq, d)
    c2 = ctx.reshape(batch * ctx_len, d)

    lat, cx = mod_parts(0)
    n1 = norm1_w[0].reshape(1, 1, d)
    n2 = norm2_w[0].reshape(1, 1, d)
    in_w = jnp.pad(ssd_in_w[0], ((0, 0), (0, SSD_IN_PAD - SSD_IN_DIM))).astype(BF16)
    proj_l = norm_mod_mm(x2, n1 * (1.0 + lat[1]), lat[0], in_w, seq, 1024, 896, "ssd_in_lat")
    proj_c = norm_mod_mm(c2, n1 * (1.0 + cx[1]), cx[0], in_w, batch * ctx_len, 1024, 896, "ssd_in_ctx")
    proj_l3 = proj_l.reshape(batch, seq, SSD_IN_PAD)
    proj_c3 = proj_c.reshape(batch, ctx_len, SSD_IN_PAD)
    dt_bias = jnp.pad(ssd_dt_bias[0].reshape(1, 2 * SSD_HEADS), ((0, 0), (0, LANES - 2 * SSD_HEADS)))
    a_neg = jnp.pad(-jnp.exp(ssd_A_log[0].astype(F32)).reshape(1, 2 * SSD_HEADS),
                    ((0, 0), (0, LANES - 2 * SSD_HEADS)))
    col_l, row_l = ssd_prep(proj_l3, dt_bias, a_neg, "ssd_prep_lat")
    col_c, row_c = ssd_prep(proj_c3, dt_bias, a_neg, "ssd_prep_ctx")
    dskip_e = jnp.repeat(ssd_D[0], SSD_HEADDIM).reshape(SSD_GROUPS, 1, SSD_HPG * SSD_HEADDIM)
    y_c, y_l = ssd_scan(proj_c3, proj_l3, ssd_conv_w[0], ssd_conv_b[0].reshape(1, CONV_DIM),
                        col_c, row_c, col_l, row_l, dskip_e)
    ssd_nw = ssd_norm_w[0].reshape(1, D_INNER)
    out_w = ssd_out_w[0].astype(BF16)
    x2 = ssd_out_mm(y_l.reshape(batch * seq, D_INNER), proj_l, ssd_nw, out_w, x2, lat[2], seq, 512,
                    "ssd_out_lat")
    c2 = ssd_out_mm(y_c.reshape(batch * ctx_len, D_INNER), proj_c, ssd_nw, out_w, c2, cx[2],
                    batch * ctx_len, 512, "ssd_out_ctx")

    rwt = moe_router_w[0].T
    w1, w3, w2 = moe_w1[0].astype(BF16), moe_w3[0].astype(BF16), moe_w2[0].astype(BF16)
    x3 = moe_block(x2.reshape(batch, seq, d), n2 * (1.0 + lat[4]), lat[3], lat[5], rwt, w1, w3, w2,
                   final_w, 1, False, "moe0_lat")
    c3 = moe_block(c2.reshape(batch, ctx_len, d), n2 * (1.0 + cx[4]), cx[3], cx[5], rwt, w1, w3, w2,
                   final_w, 8, False, "moe0_ctx")
    x2 = x3.reshape(batch * seq, d)
    c2 = c3.reshape(batch * ctx_len, d)

    lat, cx = mod_parts(1)
    n1 = norm1_w[1].reshape(1, 1, d)
    n2 = norm2_w[1].reshape(1, 1, d)
    qkv_w = attn_qkv_w[0].astype(BF16)
    qkv_l = norm_mod_mm(x2, n1 * (1.0 + lat[1]), lat[0], qkv_w, seq, 1024, 768, "qkv_lat")
    qkv_c = norm_mod_mm(c2, n1 * (1.0 + cx[1]), cx[0], qkv_w, batch * ctx_len, 1024, 768, "qkv_ctx")
    tables = _attn_tables(seq)
    q_l, k_l, v_l = qk_prep(qkv_l, attn_q_norm_w[0], attn_k_norm_w[0], tables, seq, True, 512, "qk_prep_lat")
    _, k_c, v_c = qk_prep(qkv_c, attn_q_norm_w[0], attn_k_norm_w[0], tables, ctx_len, False, ctx_len,
                          "qk_prep_ctx")
    o = attention(q_l, k_l, v_l, k_c, v_c, batch, seq, ctx_len, 128)
    x2 = res_mm(o, attn_out_w[0].astype(BF16), x2, lat[2], seq, 1024, "attn_out")

    rwt = moe_router_w[1].T
    w1, w3, w2 = moe_w1[1].astype(BF16), moe_w3[1].astype(BF16), moe_w2[1].astype(BF16)
    x3 = moe_block(x2.reshape(batch, seq, d), n2 * (1.0 + lat[4]), lat[3], lat[5], rwt, w1, w3, w2,
                   final_w, 1, True, "moe1_lat")
    return x3
```

```python
import functools
import math

import numpy as np
import jax
import jax.numpy as jnp
from jax import lax
from jax.experimental import pallas as pl
from jax.experimental.pallas import tpu as pltpu

F32 = jnp.float32
BF16 = jnp.bfloat16
HIGHEST = lax.Precision.HIGHEST

D_MODEL = 1024
GRID_W = 64
EPS = 1e-6
D_INNER = 2048
SSD_HEADDIM = 64
SSD_HEADS = 32
SSD_GROUPS = 8
SSD_HPG = 4
D_STATE = 128
CONV_W = 5
SSD_CHUNK = 128
CONV_DIM = D_INNER + 2 * SSD_GROUPS * D_STATE
SSD_IN_DIM = 2 * D_INNER + 2 * SSD_GROUPS * D_STATE + 2 * SSD_HEADS
HEAD_DIM = 64
N_Q_HEADS = 16
N_KV_HEADS = 4
Q_PER_KV = 4
D_ATTN = 1024
KV_DIM = N_KV_HEADS * HEAD_DIM
QKV_DIM = D_ATTN + 2 * KV_DIM
ROPE_THETA = 10000.0
N_EXPERTS = 16
EC_CAPACITY = 2

LANES = 128
SUBLANES = 8
VMEM_LIMIT_BYTES = 56 * 1024 * 1024


def _cparams(*sem):
    return pltpu.CompilerParams(dimension_semantics=sem, vmem_limit_bytes=VMEM_LIMIT_BYTES)


def _sigmoid(x):
    return 1.0 / (1.0 + jnp.exp(-x))


def _silu(x):
    return x * _sigmoid(x)


def _ada_kernel(c_ref, w_ref, b_ref, o_ref):
    a = _silu(c_ref[...])
    o_ref[...] = jnp.dot(a, w_ref[...], precision=HIGHEST, preferred_element_type=F32) + b_ref[...]


def ada_modulation(c_all, ada_w, ada_b):
    depth, d, n = ada_w.shape
    m = c_all.shape[0]
    tn = 1536
    return pl.pallas_call(
        _ada_kernel,
        grid=(depth, n // tn),
        in_specs=[pl.BlockSpec((m, d), lambda l, j: (0, 0)),
                  pl.BlockSpec((None, d, tn), lambda l, j: (l, 0, j)),
                  pl.BlockSpec((None, 1, tn), lambda l, j: (l, 0, j))],
        out_specs=pl.BlockSpec((None, m, tn), lambda l, j: (l, 0, j)),
        out_shape=jax.ShapeDtypeStruct((depth, m, n), F32),
        compiler_params=_cparams("parallel", "parallel"),
        name="ada_modulation",
    )(c_all, ada_w, ada_b.reshape(depth, 1, n))


def _norm_mod(x_ref, wm_ref, sh_ref):
    x = x_ref[...]
    ms = jnp.mean(x * x, axis=-1, keepdims=True)
    return (x * lax.rsqrt(ms + EPS) * wm_ref[...] + sh_ref[...]).astype(BF16)


def _norm_mod_mm_kernel(x_ref, wm_ref, sh_ref, w_ref, o_ref, a_scr):
    @pl.when(pl.program_id(1) == 0)
    def _():
        a_scr[...] = _norm_mod(x_ref, wm_ref, sh_ref)

    o_ref[...] = jnp.dot(a_scr[...], w_ref[...], preferred_element_type=F32).astype(BF16)


def _norm_mod_mm2_kernel(x_ref, wm_ref, sh_ref, w_ref, wf_ref, o_ref, of_ref, a_scr):
    @pl.when(pl.program_id(1) == 0)
    def _():
        a_scr[...] = _norm_mod(x_ref, wm_ref, sh_ref)
        of_ref[...] = jnp.dot(a_scr[...], wf_ref[...], preferred_element_type=F32)

    o_ref[...] = jnp.dot(a_scr[...], w_ref[...], preferred_element_type=F32).astype(BF16)


def norm_mod_mm(x2d, wm, sh, w, rows_per_group, tm, tn, name, w_f32out=None):
    m, k = x2d.shape
    n = w.shape[1]
    tiles_per_group = rows_per_group // tm
    in_specs = [pl.BlockSpec((tm, k), lambda i, j: (i, 0)),
                pl.BlockSpec((None, 1, k), lambda i, j: (i // tiles_per_group, 0, 0)),
                pl.BlockSpec((None, 1, k), lambda i, j: (i // tiles_per_group, 0, 0)),
                pl.BlockSpec((k, tn), lambda i, j: (0, j))]
    out_specs = [pl.BlockSpec((tm, tn), lambda i, j: (i, j))]
    out_shape = [jax.ShapeDtypeStruct((m, n), BF16)]
    args = [x2d, wm, sh, w]
    body = _norm_mod_mm_kernel
    if w_f32out is not None:
        nf = w_f32out.shape[1]
        in_specs.append(pl.BlockSpec((k, nf), lambda i, j: (0, 0)))
        out_specs.append(pl.BlockSpec((tm, nf), lambda i, j: (i, 0)))
        out_shape.append(jax.ShapeDtypeStruct((m, nf), F32))
        args.append(w_f32out)
        body = _norm_mod_mm2_kernel
    return pl.pallas_call(
        body,
        grid=(m // tm, n // tn),
        in_specs=in_specs, out_specs=out_specs, out_shape=out_shape,
        scratch_shapes=[pltpu.VMEM((tm, k), BF16)],
        compiler_params=_cparams("parallel", "arbitrary"),
        name=name,
    )(*args)


def _res_mm_kernel(a_ref, w_ref, res_ref, gate_ref, o_ref):
    acc = jnp.dot(a_ref[...], w_ref[...], preferred_element_type=F32)
    o_ref[...] = res_ref[...] + gate_ref[...] * acc


def res_mm(a, w, res, gate, rows_per_group, tm, name):
    m, k = a.shape
    n = w.shape[1]
    tiles_per_group = rows_per_group // tm
    return pl.pallas_call(
        _res_mm_kernel,
        grid=(m // tm,),
        in_specs=[pl.BlockSpec((tm, k), lambda i: (i, 0)),
                  pl.BlockSpec((k, n), lambda i: (0, 0)),
                  pl.BlockSpec((tm, n), lambda i: (i, 0)),
                  pl.BlockSpec((None, 1, n), lambda i: (i // tiles_per_group, 0, 0))],
        out_specs=pl.BlockSpec((tm, n), lambda i: (i, 0)),
        out_shape=jax.ShapeDtypeStruct((m, n), F32),
        compiler_params=_cparams("parallel"),
        name=name,
    )(a, w, res, gate)


def _ssd_out_kernel(y_ref, z_ref, nw_ref, w_ref, res_ref, gate_ref, o_ref):
    v = y_ref[...].astype(F32) * _silu(z_ref[...].astype(F32))
    ms = jnp.mean(v * v, axis=-1, keepdims=True)
    a = (v * lax.rsqrt(ms + EPS) * nw_ref[...]).astype(BF16)
    acc = jnp.dot(a, w_ref[...], preferred_element_type=F32)
    o_ref[...] = res_ref[...] + gate_ref[...] * acc


def ssd_out_mm(y, proj, norm_w, w, res, gate, rows_per_group, tm, name):
    m, k = y.shape
    n = w.shape[1]
    tiles_per_group = rows_per_group // tm
    return pl.pallas_call(
        _ssd_out_kernel,
        grid=(m // tm,),
        in_specs=[pl.BlockSpec((tm, k), lambda i: (i, 0)),
                  pl.BlockSpec((tm, k), lambda i: (i, 0)),
                  pl.BlockSpec((1, k), lambda i: (0, 0)),
                  pl.BlockSpec((k, n), lambda i: (0, 0)),
                  pl.BlockSpec((tm, n), lambda i: (i, 0)),
                  pl.BlockSpec((None, 1, n), lambda i: (i // tiles_per_group, 0, 0))],
        out_specs=pl.BlockSpec((tm, n), lambda i: (i, 0)),
        out_shape=jax.ShapeDtypeStruct((m, n), F32),
        compiler_params=_cparams("parallel"),
        name=name,
    )(y, proj, norm_w, w, res, gate)


def _ssd_prep_kernel(dtr_ref, bias_ref, a_ref, scol_ref, srow_ref, col_ref, row_ref):
    seq = dtr_ref.shape[0]
    x = dtr_ref[...] + bias_ref[...]
    dt = jnp.maximum(x, 0.0) + jnp.log1p(jnp.exp(-jnp.abs(x)))
    a = dt * a_ref[...]
    t_i = lax.broadcasted_iota(jnp.int32, (SSD_CHUNK, SSD_CHUNK), 0)
    s_i = lax.broadcasted_iota(jnp.int32, (SSD_CHUNK, SSD_CHUNK), 1)
    tri_lo = (s_i <= t_i).astype(F32)
    tri_up = (s_i >= t_i).astype(F32)
    fwd_lane = lax.broadcasted_iota(jnp.int32, (SSD_CHUNK, LANES), 1) < SSD_HEADS
    chunks = []
    for c in range(seq // SSD_CHUNK):
        ac = a[c * SSD_CHUNK:(c + 1) * SSD_CHUNK]
        cf = jnp.dot(tri_lo, ac, precision=HIGHEST, preferred_element_type=F32)
        cb = jnp.dot(tri_up, ac, precision=HIGHEST, preferred_element_type=F32)
        chunks.append(jnp.where(fwd_lane, cf, cb))
    cum = jnp.concatenate(chunks, axis=0)
    both = jnp.concatenate([dt, cum], axis=1)
    hi = both.astype(BF16)
    rest = both - hi.astype(F32)
    mid = rest.astype(BF16)
    lo = (rest - mid.astype(F32)).astype(BF16)
    scol = scol_ref[...]
    col_ref[...] = (jnp.dot(hi, scol, preferred_element_type=F32)
                    + jnp.dot(mid, scol, preferred_element_type=F32)
                    + jnp.dot(lo, scol, preferred_element_type=F32))
    row_ref[...] = lax.dot_general(srow_ref[...], cum, (((1,), (1,)), ((), ())),
                                   precision=HIGHEST, preferred_element_type=F32)


def _ssd_select_tables():
    scol = np.zeros((2 * LANES, SSD_GROUPS * LANES), np.float32)
    srow = np.zeros((SSD_GROUPS * SUBLANES, LANES), np.float32)
    for g in range(SSD_GROUPS):
        for d in range(2):
            for r in range(SSD_HPG):
                lane = d * SSD_HEADS + SSD_HPG * g + r
                scol[lane, g * LANES + d * SSD_HPG + r] = 1.0
                scol[LANES + lane, g * LANES + 2 * SSD_HPG + d * SSD_HPG + r] = 1.0
                srow[g * SUBLANES + d * SSD_HPG + r, lane] = 1.0
    return jnp.asarray(scol, dtype=BF16), jnp.asarray(srow)


def ssd_prep(dt_raw3, dt_bias, a_neg, name):
    b, seq, _ = dt_raw3.shape
    scol, srow = _ssd_select_tables()
    return pl.pallas_call(
        _ssd_prep_kernel,
        grid=(b,),
        in_specs=[pl.BlockSpec((None, seq, LANES), lambda i: (i, 0, 0)),
                  pl.BlockSpec((1, LANES), lambda i: (0, 0)),
                  pl.BlockSpec((1, LANES), lambda i: (0, 0)),
                  pl.BlockSpec(scol.shape, lambda i: (0, 0)),
                  pl.BlockSpec(srow.shape, lambda i: (0, 0))],
        out_specs=[pl.BlockSpec((None, seq, SSD_GROUPS * LANES), lambda i: (i, 0, 0)),
                   pl.BlockSpec((None, SSD_GROUPS * SUBLANES, seq), lambda i: (i, 0, 0))],
        out_shape=[jax.ShapeDtypeStruct((b, seq, SSD_GROUPS * LANES), F32),
                   jax.ShapeDtypeStruct((b, SSD_GROUPS * SUBLANES, seq), F32)],
        compiler_params=_cparams("parallel"),
        name=name,
    )(dt_raw3, dt_bias, a_neg, scol, srow)


CONV_HALO = 2 * SUBLANES
CONV_TAPS_OFF_CENTRE = (0, 1, 3, 4)
SSD_UNROLL = 4


def _ssd_scan_kernel(xc_ref, bc_ref, cc_ref, xl_ref, bl_ref, cl_ref,
                     cwx_ref, cwb_ref, cwc_ref, cbx_ref, cbb_ref, cbc_ref,
                     colc_ref, rowc_ref, coll_ref, rowl_ref, dsk_ref,
                     yc_ref, yl_ref,
                     pad_scr, xs_scr, bt_scr, cs_scr, y_scr, xdt_scr, xdw_scr, ecum_scr, seg_scr, ed_scr,
                     hf_scr, hb_scr):
    q = SSD_CHUNK
    width = SSD_HPG * SSD_HEADDIM
    win_rows = q + 2 * CONV_HALO
    dsk = dsk_ref[...]

    r_i = lax.broadcasted_iota(jnp.int32, (len(CONV_TAPS_OFF_CENTRE) * q, win_rows), 0)
    j_i = lax.broadcasted_iota(jnp.int32, (len(CONV_TAPS_OFF_CENTRE) * q, win_rows), 1)
    tap_of_row = jnp.zeros_like(r_i)
    for n, k in enumerate(CONV_TAPS_OFF_CENTRE):
        tap_of_row = jnp.where(r_i // q == n, k, tap_of_row)
    shift_mat = jnp.where(j_i == r_i % q + CONV_HALO - CONV_W // 2 + tap_of_row, 1.0, 0.0).astype(BF16)

    def conv_silu(parts, w, bias, seq, store):
        zeros = jnp.zeros((CONV_HALO, width), BF16)
        pad_scr[0:CONV_HALO, :] = zeros
        pad_scr[CONV_HALO + seq:2 * CONV_HALO + seq, :] = zeros

        def copy_body(c, carry):
            base = pl.multiple_of(c * q, q)
            dst = pl.multiple_of(base + CONV_HALO, CONV_HALO)
            for raw_ref, lane0 in parts:
                pad_scr[pl.ds(dst, q), lane0:lane0 + raw_ref.shape[1]] = raw_ref[pl.ds(base, q), :]
            return carry

        lax.fori_loop(0, seq // q, copy_body, 0)

        def conv_body(c, carry):
            base = pl.multiple_of(c * q, q)
            win = pad_scr[pl.ds(base, win_rows), :]
            shifted = jnp.dot(shift_mat, win, preferred_element_type=F32)
            acc = bias + win[CONV_HALO:CONV_HALO + q].astype(F32) * w[CONV_W // 2:CONV_W // 2 + 1]
            for n, k in enumerate(CONV_TAPS_OFF_CENTRE):
                acc = acc + shifted[n * q:(n + 1) * q] * w[k:k + 1]
            store(base, _silu(acc))
            return carry

        lax.fori_loop(0, seq // q, conv_body, 0, unroll=min(SSD_UNROLL, seq // q))

    def x_store(base, v):
        xs_scr[pl.ds(base, q), :] = v
        y_scr[pl.ds(base, q), :] = dsk * v

    def bc_store(base, v):
        bt_scr[:, pl.ds(base, q)] = v[:, 0:D_STATE].T.astype(BF16)
        cs_scr[pl.ds(base, q), :] = v[:, D_STATE:2 * D_STATE].astype(BF16)

    w_bc = jnp.concatenate([cwb_ref[...], cwc_ref[...]], axis=1)
    b_bc = jnp.concatenate([cbb_ref[...], cbc_ref[...]], axis=1)

    t_i = lax.broadcasted_iota(jnp.int32, (q, q), 0)
    s_i = lax.broadcasted_iota(jnp.int32, (q, q), 1)
    head_of_lane = lax.broadcasted_iota(jnp.int32, (q, width), 1) // SSD_HEADDIM

    def per_head(parts):
        out = parts[SSD_HPG - 1]
        for r in range(SSD_HPG - 2, -1, -1):
            out = jnp.where(head_of_lane == r, parts[r], out)
        return out

    def run_seq(x_ref, b_ref, c_ref, col_ref, row_ref, y_ref):
        seq = x_ref.shape[0]
        nchunks = seq // q
        conv_silu([(x_ref, 0)], cwx_ref[...], cbx_ref[...], seq, x_store)
        conv_silu([(b_ref, 0), (c_ref, D_STATE)], w_bc, b_bc, seq, bc_store)

        def decay_terms(c, d):
            mask = (s_i <= t_i) if d == 0 else (s_i >= t_i)
            edge = q - 1 if d == 0 else 0
            base = pl.multiple_of(c * q, q)
            cp = col_ref[pl.ds(base, q), :]
            rp = row_ref[:, pl.ds(base, q)]
            dt_cols = [cp[:, SSD_HPG * d + r:SSD_HPG * d + r + 1] for r in range(SSD_HPG)]
            cum_cols = [cp[:, 2 * SSD_HPG + SSD_HPG * d + r:2 * SSD_HPG + SSD_HPG * d + r + 1]
                        for r in range(SSD_HPG)]
            cum_e = per_head(cum_cols)
            xdt = xs_scr[pl.ds(base, q), :] * per_head(dt_cols)
            cum_edge = cum_e[edge:edge + 1, :]
            xdt_scr[d, pl.ds(base, q), :] = xdt.astype(BF16)
            xdw_scr[d, pl.ds(base, q), :] = (xdt * jnp.exp(cum_edge - cum_e)).astype(BF16)
            ecum_scr[d, pl.ds(base, q), :] = jnp.exp(cum_e)
            ed_scr[d, pl.ds(pl.multiple_of(c * SUBLANES, SUBLANES), SUBLANES), :] = jnp.broadcast_to(
                jnp.exp(cum_edge), (SUBLANES, width))
            for r in range(SSD_HPG):
                row = rp[SSD_HPG * d + r:SSD_HPG * d + r + 1, :]
                seg = jnp.exp(jnp.where(mask, cum_cols[r] - row, -1e30))
                seg_scr[SSD_HPG * d + r, pl.ds(base, q), :] = seg.astype(BF16)

        def prep_body(c, carry):
            decay_terms(c, 0)
            decay_terms(c, 1)
            return carry

        lax.fori_loop(0, nchunks, prep_body, 0, unroll=min(SSD_UNROLL, nchunks))

        def chunk(c, d, h_scr):
            base = pl.multiple_of(c * q, q)
            bq_t = bt_scr[:, pl.ds(base, q)]
            cq = cs_scr[pl.ds(base, q), :]
            cb = jnp.dot(cq, bq_t, preferred_element_type=F32).astype(BF16)
            ms = [cb * seg_scr[SSD_HPG * d + r, pl.ds(base, q), :] for r in range(SSD_HPG)]
            stacked = jnp.dot(jnp.concatenate(ms, axis=0), xdt_scr[d, pl.ds(base, q), :],
                              preferred_element_type=F32)
            y = per_head([stacked[r * q:(r + 1) * q] for r in range(SSD_HPG)])
            h_t = h_scr[...]
            y = y + jnp.dot(cq, h_t.astype(BF16), preferred_element_type=F32) * ecum_scr[d, pl.ds(base, q), :]
            edge_decay = ed_scr[d, pl.ds(pl.multiple_of(c * SUBLANES, SUBLANES), SUBLANES), :][0:1]
            h_scr[...] = h_t * edge_decay + jnp.dot(bq_t, xdw_scr[d, pl.ds(base, q), :],
                                                    preferred_element_type=F32)
            y_scr[pl.ds(base, q), :] += y

        def scan_body(i, carry):
            chunk(i, 0, hf_scr)
            chunk(nchunks - 1 - i, 1, hb_scr)
            return carry

        lax.fori_loop(0, nchunks, scan_body, 0, unroll=min(SSD_UNROLL, nchunks))

        def out_body(c, carry):
            base = pl.multiple_of(c * q, q)
            y_ref[pl.ds(base, q), :] = y_scr[pl.ds(base, q), :].astype(BF16)
            return carry

        lax.fori_loop(0, nchunks, out_body, 0)

    hf_scr[...] = jnp.zeros(hf_scr.shape, F32)
    hb_scr[...] = jnp.zeros(hb_scr.shape, F32)
    run_seq(xc_ref, bc_ref, cc_ref, colc_ref, rowc_ref, yc_ref)
    run_seq(xl_ref, bl_ref, cl_ref, coll_ref, rowl_ref, yl_ref)


def ssd_scan(proj_c, proj_l, conv_w, conv_b, col_c, row_c, col_l, row_l, dskip_e):
    b, lc, _ = proj_c.shape
    ll = proj_l.shape[1]
    width = SSD_HPG * SSD_HEADDIM
    x0 = D_INNER // width
    b0 = (2 * D_INNER) // D_STATE
    c0 = b0 + SSD_GROUPS
    wx0, wb0, wc0 = 0, D_INNER // D_STATE, D_INNER // D_STATE + SSD_GROUPS

    def seq_specs(seq):
        return [pl.BlockSpec((None, seq, width), lambda i, g: (i, 0, x0 + g)),
                pl.BlockSpec((None, seq, D_STATE), lambda i, g: (i, 0, b0 + g)),
                pl.BlockSpec((None, seq, D_STATE), lambda i, g: (i, 0, c0 + g))]

    def par_specs(rows):
        return [pl.BlockSpec((rows, width), lambda i, g: (0, wx0 + g)),
                pl.BlockSpec((rows, D_STATE), lambda i, g: (0, wb0 + g)),
                pl.BlockSpec((rows, D_STATE), lambda i, g: (0, wc0 + g))]

    def pack_specs(seq):
        return [pl.BlockSpec((None, seq, LANES), lambda i, g: (i, 0, g)),
                pl.BlockSpec((None, SUBLANES, seq), lambda i, g: (i, g, 0))]

    return pl.pallas_call(
        _ssd_scan_kernel,
        grid=(b, SSD_GROUPS),
        in_specs=(seq_specs(lc) + seq_specs(ll) + par_specs(CONV_W) + par_specs(1)
                  + pack_specs(lc) + pack_specs(ll)
                  + [pl.BlockSpec((None, 1, width), lambda i, g: (g, 0, 0))]),
        out_specs=[pl.BlockSpec((None, lc, width), lambda i, g: (i, 0, g)),
                   pl.BlockSpec((None, ll, width), lambda i, g: (i, 0, g))],
        out_shape=[jax.ShapeDtypeStruct((b, lc, D_INNER), BF16),
                   jax.ShapeDtypeStruct((b, ll, D_INNER), BF16)],
        scratch_shapes=[pltpu.VMEM((ll + 2 * CONV_HALO, width), BF16),
                        pltpu.VMEM((ll, width), F32),
                        pltpu.VMEM((D_STATE, ll), BF16),
                        pltpu.VMEM((ll, D_STATE), BF16),
                        pltpu.VMEM((ll, width), F32),
                        pltpu.VMEM((2, ll, width), BF16),
                        pltpu.VMEM((2, ll, width), BF16),
                        pltpu.VMEM((2, ll, width), F32),
                        pltpu.VMEM((2 * SSD_HPG, ll, SSD_CHUNK), BF16),
                        pltpu.VMEM((2, ll // SSD_CHUNK * SUBLANES, width), F32),
                        pltpu.VMEM((D_STATE, width), F32), pltpu.VMEM((D_STATE, width), F32)],
        compiler_params=_cparams("parallel", "parallel"),
        name="ssd_scan",
    )(proj_c, proj_c, proj_c, proj_l, proj_l, proj_l,
      conv_w, conv_w, conv_w, conv_b, conv_b, conv_b,
      col_c, row_c, col_l, row_l, dskip_e)


def _qk_prep_kernel(qkv_ref, gmat_ref, qw_ref, kw_ref, rope_c_ref, rope_a_ref, rope_b_ref, rep_ref,
                    q_ref, k_ref, v_ref, *, use_rope):
    blk = Q_PER_KV * HEAD_DIM
    gmat = gmat_ref[...]

    def norm_rope(x, w):
        ms = jnp.dot(x * x, gmat, precision=HIGHEST, preferred_element_type=F32)
        xn = x * lax.rsqrt(ms + EPS) * w
        if use_rope:
            half = HEAD_DIM // 2
            xn = (xn * rope_c_ref[...] + pltpu.roll(xn, blk - half, 1) * rope_a_ref[...]
                  + pltpu.roll(xn, half, 1) * rope_b_ref[...])
        return xn

    scale = 1.0 / math.sqrt(HEAD_DIM)
    for j in range(D_ATTN // blk):
        xq = norm_rope(qkv_ref[:, j * blk:(j + 1) * blk].astype(F32), qw_ref[...])
        q_ref[:, j * blk:(j + 1) * blk] = (xq * scale).astype(BF16)
    xk = norm_rope(qkv_ref[:, D_ATTN:D_ATTN + KV_DIM].astype(F32), kw_ref[...]).astype(BF16)
    xv = qkv_ref[:, D_ATTN + KV_DIM:D_ATTN + 2 * KV_DIM]
    for kv in range(N_KV_HEADS):
        rep = rep_ref[kv]
        k_ref[:, kv * blk:(kv + 1) * blk] = jnp.dot(xk, rep, preferred_element_type=F32).astype(BF16)
        v_ref[:, kv * blk:(kv + 1) * blk] = jnp.dot(xv, rep, preferred_element_type=F32).astype(BF16)


def _attn_tables(seq):
    rows = seq // GRID_W
    row = np.repeat(np.arange(rows), GRID_W).astype(np.float32)
    col = np.tile(np.arange(GRID_W), rows).astype(np.float32)
    axis_dim = HEAD_DIM // 2
    inv_freq = jnp.asarray(ROPE_THETA, F32) ** (-jnp.arange(0, axis_dim, 2, dtype=F32) / axis_dim)
    ang = jnp.concatenate([jnp.asarray(row)[:, None] * inv_freq, jnp.asarray(col)[:, None] * inv_freq], axis=-1)
    cos, sin = jnp.cos(ang), jnp.sin(ang)
    zero = jnp.zeros_like(sin)
    reps = Q_PER_KV
    rope_c = jnp.tile(jnp.concatenate([cos, cos], axis=-1), (1, reps))
    rope_a = jnp.tile(jnp.concatenate([-sin, zero], axis=-1), (1, reps))
    rope_b = jnp.tile(jnp.concatenate([zero, sin], axis=-1), (1, reps))
    return rope_c, rope_a, rope_b


def _attn_consts():
    blk = Q_PER_KV * HEAD_DIM
    head = np.arange(blk) // HEAD_DIM
    gmat = (head[:, None] == head[None, :]).astype(np.float32) / HEAD_DIM
    rep = np.zeros((N_KV_HEADS, KV_DIM, blk), np.float32)
    for kv in range(N_KV_HEADS):
        for j in range(blk):
            rep[kv, kv * HEAD_DIM + j % HEAD_DIM, j] = 1.0
    return jnp.asarray(gmat), jnp.asarray(rep, dtype=BF16)


def qk_prep(qkv, q_norm_w, k_norm_w, tables, seq, use_rope, tl, name):
    m = qkv.shape[0]
    blk = Q_PER_KV * HEAD_DIM
    gmat, rep = _attn_consts()
    qw = jnp.tile(q_norm_w.reshape(1, HEAD_DIM), (1, Q_PER_KV))
    kw = jnp.tile(k_norm_w.reshape(1, HEAD_DIM), (1, N_KV_HEADS))
    tiles_per_seq = seq // tl
    tab_spec = pl.BlockSpec((tl, blk), lambda i: (i % tiles_per_seq, 0))
    out_w = N_KV_HEADS * blk
    return pl.pallas_call(
        functools.partial(_qk_prep_kernel, use_rope=use_rope),
        grid=(m // tl,),
        in_specs=[pl.BlockSpec((tl, QKV_DIM), lambda i: (i, 0)),
                  pl.BlockSpec((blk, blk), lambda i: (0, 0)),
                  pl.BlockSpec((1, blk), lambda i: (0, 0)),
                  pl.BlockSpec((1, blk), lambda i: (0, 0)),
                  tab_spec, tab_spec, tab_spec,
                  pl.BlockSpec(rep.shape, lambda i: (0, 0, 0))],
        out_specs=[pl.BlockSpec((tl, D_ATTN), lambda i: (i, 0)),
                   pl.BlockSpec((tl, out_w), lambda i: (i, 0)),
                   pl.BlockSpec((tl, out_w), lambda i: (i, 0))],
        out_shape=[jax.ShapeDtypeStruct((m, D_ATTN), BF16),
                   jax.ShapeDtypeStruct((m, out_w), BF16),
                   jax.ShapeDtypeStruct((m, out_w), BF16)],
        compiler_params=_cparams("parallel"),
        name=name,
    )(qkv, gmat, qw, kw, *tables, rep)


def _attn_kernel(q_ref, kl_ref, vl_ref, kc_ref, vc_ref, o_ref):
    tq = q_ref.shape[0]
    blk = Q_PER_KV * HEAD_DIM
    head_of_lane = lax.broadcasted_iota(jnp.int32, (tq, blk), 1) // HEAD_DIM
    nt = (((1,), (1,)), ((), ()))
    for kv in range(N_KV_HEADS):
        cols = slice(kv * blk, (kv + 1) * blk)
        q4 = q_ref[:, cols]
        zero = jnp.zeros_like(q4)
        qs = jnp.concatenate([jnp.where(head_of_lane == g, q4, zero) for g in range(Q_PER_KV)], axis=0)
        s_l = lax.dot_general(qs, kl_ref[:, cols], nt, preferred_element_type=F32)
        s_c = lax.dot_general(qs, kc_ref[:, cols], nt, preferred_element_type=F32)
        m = jnp.maximum(jnp.max(s_l, axis=-1, keepdims=True), jnp.max(s_c, axis=-1, keepdims=True))
        p_l = jnp.exp(s_l - m)
        p_c = jnp.exp(s_c - m)
        denom = jnp.sum(p_l, axis=-1, keepdims=True) + jnp.sum(p_c, axis=-1, keepdims=True)
        o = (jnp.dot(p_l.astype(BF16), vl_ref[:, cols], preferred_element_type=F32)
             + jnp.dot(p_c.astype(BF16), vc_ref[:, cols], preferred_element_type=F32)) / denom
        out = o[(Q_PER_KV - 1) * tq:Q_PER_KV * tq]
        for g in range(Q_PER_KV - 2, -1, -1):
            out = jnp.where(head_of_lane == g, o[g * tq:(g + 1) * tq], out)
        o_ref[:, cols] = out.astype(BF16)


def attention(q, k_lat, v_lat, k_ctx, v_ctx, batch, seq, ctx_len, tq):
    width = q.shape[1]
    nq = seq // tq
    return pl.pallas_call(
        _attn_kernel,
        grid=(batch, nq),
        in_specs=[pl.BlockSpec((tq, width), lambda b, i: (b * nq + i, 0)),
                  pl.BlockSpec((seq, width), lambda b, i: (b, 0)),
                  pl.BlockSpec((seq, width), lambda b, i: (b, 0)),
                  pl.BlockSpec((ctx_len, width), lambda b, i: (b, 0)),
                  pl.BlockSpec((ctx_len, width), lambda b, i: (b, 0))],
        out_specs=pl.BlockSpec((tq, width), lambda b, i: (b * nq + i, 0)),
        out_shape=jax.ShapeDtypeStruct((batch * seq, width), BF16),
        compiler_params=_cparams("parallel", "parallel"),
        name="attention",
    )(q, k_lat, v_lat, k_ctx, v_ctx)


def _router_kernel(x_ref, wm_ref, sh_ref, rw_ref, hn_ref, pos_ref, gate_ref, *, cap):
    tokens = x_ref.shape[0]
    x = x_ref[...]
    ms = jnp.mean(x * x, axis=-1, keepdims=True)
    hn = x * lax.rsqrt(ms + EPS) * wm_ref[...] + sh_ref[...]
    hn_ref[...] = hn.astype(BF16)
    logits = lax.dot_general(rw_ref[...], hn, (((1,), (1,)), ((), ())),
                             precision=HIGHEST, preferred_element_type=F32)
    e = jnp.exp(logits - jnp.max(logits, axis=0, keepdims=True))
    aff = e / jnp.sum(e, axis=0, keepdims=True)
    gate_ref[...] = aff
    bits = pltpu.bitcast(aff, jnp.int32)

    def count(flags):
        return jnp.sum(jnp.where(flags, 1.0, 0.0), axis=1, keepdims=True)

    def search(i, thr):
        cand = thr | jnp.left_shift(jnp.int32(1), 30 - i)
        return jnp.where(count(bits >= cand) >= cap, cand, thr)

    thr = lax.fori_loop(0, 31, search, jnp.zeros((N_EXPERTS, 1), jnp.int32))
    above = bits > thr
    tied = bits == thr
    need = cap - count(above)

    t_i = lax.broadcasted_iota(jnp.int32, (LANES, LANES), 0)
    s_i = lax.broadcasted_iota(jnp.int32, (LANES, LANES), 1)
    tri = (t_i <= s_i).astype(BF16)

    def lane_cumsum(flags):
        carry = jnp.zeros((N_EXPERTS, 1), F32)
        parts = []
        for c in range(tokens // LANES):
            part = jnp.dot(flags[:, c * LANES:(c + 1) * LANES].astype(BF16), tri,
                           preferred_element_type=F32) + carry
            parts.append(part)
            carry = part[:, LANES - 1:LANES]
        return jnp.concatenate(parts, axis=1)

    tie_rank = lane_cumsum(jnp.where(tied, 1.0, 0.0))
    sel = above | (tied & (tie_rank <= need))
    slot = lane_cumsum(jnp.where(sel, 1.0, 0.0)) - 1.0
    pos_ref[...] = jnp.where(sel, slot, -1.0).astype(jnp.int32)


def moe_router(x3, wm, sh, router_wt, cap, name):
    b, tokens, d = x3.shape
    per_batch = wm.shape[0] > 1
    mod_spec = pl.BlockSpec((None, 1, d), (lambda i: (i, 0, 0)) if per_batch else (lambda i: (0, 0, 0)))
    return pl.pallas_call(
        functools.partial(_router_kernel, cap=cap),
        grid=(b,),
        in_specs=[pl.BlockSpec((None, tokens, d), lambda i: (i, 0, 0)),
                  mod_spec, mod_spec,
                  pl.BlockSpec((N_EXPERTS, d), lambda i: (0, 0))],
        out_specs=[pl.BlockSpec((None, tokens, d), lambda i: (i, 0, 0)),
                   pl.BlockSpec((None, N_EXPERTS, tokens), lambda i: (i, 0, 0)),
                   pl.BlockSpec((None, N_EXPERTS, tokens), lambda i: (i, 0, 0))],
        out_shape=[jax.ShapeDtypeStruct((b, tokens, d), BF16),
                   jax.ShapeDtypeStruct((b, N_EXPERTS, tokens), jnp.int32),
                   jax.ShapeDtypeStruct((b, N_EXPERTS, tokens), F32)],
        compiler_params=_cparams("parallel"),
        name=name,
    )(x3, wm, sh, router_wt)


def _expert_kernel(hn_ref, pos_ref, gate_ref, w1_ref, w3_ref, w2_ref, g2_ref, x_ref, fw_ref, o_ref,
                   *, cap, final_norm):
    nb, tokens, _ = hn_ref.shape
    e = pl.program_id(1)

    @pl.when(e == 0)
    def _():
        o_ref[...] = x_ref[...]

    slot_i = lax.broadcasted_iota(jnp.int32, (cap, tokens), 0)
    onehots, gathered, gates = [], [], []
    for n in range(nb):
        hit = slot_i == pos_ref[n]
        onehot = jnp.where(hit, 1.0, 0.0).astype(BF16)
        onehots.append(onehot)
        gates.append(jnp.sum(jnp.where(hit, gate_ref[n], 0.0), axis=1, keepdims=True))
        gathered.append(jnp.dot(onehot, hn_ref[n], preferred_element_type=F32).astype(BF16))
    xg = jnp.concatenate(gathered, axis=0) if nb > 1 else gathered[0]
    a = jnp.dot(xg, w1_ref[...], preferred_element_type=F32)
    g = jnp.dot(xg, w3_ref[...], preferred_element_type=F32)
    y = jnp.dot((_silu(a) * g).astype(BF16), w2_ref[...], preferred_element_type=F32)
    for n in range(nb):
        yn = (y[n * cap:(n + 1) * cap] * gates[n] * g2_ref[n]).astype(BF16)
        o_ref[n] += lax.dot_general(onehots[n], yn, (((0,), (0,)), ((), ())),
                                    preferred_element_type=F32)

    if final_norm:
        @pl.when(e == pl.num_programs(1) - 1)
        def _():
            for n in range(nb):
                v = o_ref[n]
                ms = jnp.mean(v * v, axis=-1, keepdims=True)
                o_ref[n] = v * lax.rsqrt(ms + EPS) * fw_ref[...]


def moe_experts(hn, pos, gate, w1, w3, w2, g2, x3, final_w, cap, nb, final_norm, name):
    b, tokens, d = hn.shape
    g2 = jnp.broadcast_to(g2, (b, 1, d))
    g2_spec = pl.BlockSpec((nb, 1, d), lambda i, e: (i, 0, 0))
    pos4 = pos.reshape(b, N_EXPERTS, 1, tokens)
    gate4 = gate.reshape(b, N_EXPERTS, 1, tokens)
    sel_spec = pl.BlockSpec((nb, None, 1, tokens), lambda i, e: (i, e, 0, 0))
    w_spec = pl.BlockSpec((None, d, d), lambda i, e: (e, 0, 0))
    tok_spec = pl.BlockSpec((nb, tokens, d), lambda i, e: (i, 0, 0), pipeline_mode=pl.Buffered(1))
    return pl.pallas_call(
        functools.partial(_expert_kernel, cap=cap, final_norm=final_norm),
        grid=(b // nb, N_EXPERTS),
        in_specs=[tok_spec, sel_spec, sel_spec, w_spec, w_spec, w_spec, g2_spec, tok_spec,
                  pl.BlockSpec((1, d), lambda i, e: (0, 0))],
        out_specs=tok_spec,
        out_shape=jax.ShapeDtypeStruct((b, tokens, d), F32),
        compiler_params=_cparams("parallel", "arbitrary"),
        name=name,
    )(hn, pos4, gate4, w1, w3, w2, g2, x3, final_w)


def moe_block(x3, wm, sh, g2, router_wt, w1, w3, w2, final_w, nb, final_norm, name):
    tokens = x3.shape[1]
    cap = EC_CAPACITY * tokens // N_EXPERTS
    hn, pos, gate = moe_router(x3, wm, sh, router_wt, cap, name + "_router")
    return moe_experts(hn, pos, gate, w1, w3, w2, g2, x3, final_w, cap, nb, final_norm, name + "_experts")


def kernel(x, c, ctx, c_ctx, ada_w, ada_b, norm1_w, norm2_w, ssd_in_w, ssd_conv_w, ssd_conv_b, ssd_dt_bias, ssd_A_log, ssd_D, ssd_norm_w, ssd_out_w, attn_qkv_w, attn_q_norm_w, attn_k_norm_w, attn_out_w, moe_router_w, moe_w1, moe_w3, moe_w2, final_norm_w):
    batch, seq, d = x.shape
    ctx_len = ctx.shape[1]
    depth = ada_w.shape[0]
    assert depth == 2 and d == D_MODEL

    mod_rows = -(-(batch + 1) // SUBLANES) * SUBLANES
    c_all = jnp.zeros((mod_rows, d), F32).at[:batch].set(c).at[batch].set(c_ctx)
    mod = ada_modulation(c_all, ada_w, ada_b)

    def mod_parts(layer):
        parts = jnp.split(mod[layer], 6, axis=-1)
        lat = [p[:batch].reshape(batch, 1, d) for p in parts]
        cx = [p[batch:batch + 1].reshape(1, 1, d) for p in parts]
        return lat, cx

    final_w = final_norm_w.reshape(1, d)
    x2 = x.reshape(batch * seq, d)
    c2 = ctx.reshape(batch * ctx_len, d)

    lat, cx = mod_parts(0)
    n1 = norm1_w[0].reshape(1, 1, d)
    n2 = norm2_w[0].reshape(1, 1, d)
    n_main = SSD_IN_DIM - 2 * SSD_HEADS
    in_w = ssd_in_w[0][:, :n_main].astype(BF16)
    in_w_dt = jnp.pad(ssd_in_w[0][:, n_main:], ((0, 0), (0, LANES - 2 * SSD_HEADS))).astype(BF16)
    proj_l, dtr_l = norm_mod_mm(x2, n1 * (1.0 + lat[1]), lat[0], in_w, seq, 1024, 768, "ssd_in_lat", in_w_dt)
    proj_c, dtr_c = norm_mod_mm(c2, n1 * (1.0 + cx[1]), cx[0], in_w, batch * ctx_len, 1024, 768, "ssd_in_ctx",
                                in_w_dt)
    proj_l3 = proj_l.reshape(batch, seq, n_main)
    proj_c3 = proj_c.reshape(batch, ctx_len, n_main)
    dt_bias = jnp.pad(ssd_dt_bias[0].reshape(1, 2 * SSD_HEADS), ((0, 0), (0, LANES - 2 * SSD_HEADS)))
    a_neg = jnp.pad(-jnp.exp(ssd_A_log[0].astype(F32)).reshape(1, 2 * SSD_HEADS),
                    ((0, 0), (0, LANES - 2 * SSD_HEADS)))
    col_l, row_l = ssd_prep(dtr_l.reshape(batch, seq, LANES), dt_bias, a_neg, "ssd_prep_lat")
    col_c, row_c = ssd_prep(dtr_c.reshape(batch, ctx_len, LANES), dt_bias, a_neg, "ssd_prep_ctx")
    dskip_e = jnp.repeat(ssd_D[0], SSD_HEADDIM).reshape(SSD_GROUPS, 1, SSD_HPG * SSD_HEADDIM)
    y_c, y_l = ssd_scan(proj_c3, proj_l3, ssd_conv_w[0], ssd_conv_b[0].reshape(1, CONV_DIM),
                        col_c, row_c, col_l, row_l, dskip_e)
    ssd_nw = ssd_norm_w[0].reshape(1, D_INNER)
    out_w = ssd_out_w[0].astype(BF16)
    x2 = ssd_out_mm(y_l.reshape(batch * seq, D_INNER), proj_l, ssd_nw, out_w, x2, lat[2], seq, 512,
                    "ssd_out_lat")
    c2 = ssd_out_mm(y_c.reshape(batch * ctx_len, D_INNER), proj_c, ssd_nw, out_w, c2, cx[2],
                    batch * ctx_len, 512, "ssd_out_ctx")

    rwt = moe_router_w[0].T
    w1, w3, w2 = moe_w1[0].astype(BF16), moe_w3[0].astype(BF16), moe_w2[0].astype(BF16)
    x3 = moe_block(x2.reshape(batch, seq, d), n2 * (1.0 + lat[4]), lat[3], lat[5], rwt, w1, w3, w2,
                   final_w, 1, False, "moe0_lat")
    c3 = moe_block(c2.reshape(batch, ctx_len, d), n2 * (1.0 + cx[4]), cx[3], cx[5], rwt, w1, w3, w2,
                   final_w, math.gcd(batch, 8), False, "moe0_ctx")
    x2 = x3.reshape(batch * seq, d)
    c2 = c3.reshape(batch * ctx_len, d)

    lat, cx = mod_parts(1)
    n1 = norm1_w[1].reshape(1, 1, d)
    n2 = norm2_w[1].reshape(1, 1, d)
    qkv_w = attn_qkv_w[0].astype(BF16)
    qkv_l, = norm_mod_mm(x2, n1 * (1.0 + lat[1]), lat[0], qkv_w, seq, 1024, 768, "qkv_lat")
    qkv_c, = norm_mod_mm(c2, n1 * (1.0 + cx[1]), cx[0], qkv_w, batch * ctx_len, 1024, 768, "qkv_ctx")
    tables = _attn_tables(seq)
    q_l, k_l, v_l = qk_prep(qkv_l, attn_q_norm_w[0], attn_k_norm_w[0], tables, seq, True, 512, "qk_prep_lat")
    _, k_c, v_c = qk_prep(qkv_c, attn_q_norm_w[0], attn_k_norm_w[0], tables, ctx_len, False, ctx_len,
                          "qk_prep_ctx")
    o = attention(q_l, k_l, v_l, k_c, v_c, batch, seq, ctx_len, 128)
    x2 = res_mm(o, attn_out_w[0].astype(BF16), x2, lat[2], seq, 1024, "attn_out")

    rwt = moe_router_w[1].T
    w1, w3, w2 = moe_w1[1].astype(BF16), moe_w3[1].astype(BF16), moe_w2[1].astype(BF16)
    x3 = moe_block(x2.reshape(batch, seq, d), n2 * (1.0 + lat[4]), lat[3], lat[5], rwt, w1, w3, w2,
                   final_w, 1, True, "moe1_lat")
    return x3
```

```python
import functools
import math

import numpy as np
import jax
import jax.numpy as jnp
from jax import lax
from jax.experimental import pallas as pl
from jax.experimental.pallas import tpu as pltpu

F32 = jnp.float32
BF16 = jnp.bfloat16
HIGHEST = lax.Precision.HIGHEST

D_MODEL = 1024
GRID_W = 64
EPS = 1e-6
D_INNER = 2048
SSD_HEADDIM = 64
SSD_HEADS = 32
SSD_GROUPS = 8
SSD_HPG = 4
D_STATE = 128
CONV_W = 5
SSD_CHUNK = 128
CONV_DIM = D_INNER + 2 * SSD_GROUPS * D_STATE
SSD_IN_DIM = 2 * D_INNER + 2 * SSD_GROUPS * D_STATE + 2 * SSD_HEADS
HEAD_DIM = 64
N_Q_HEADS = 16
N_KV_HEADS = 4
Q_PER_KV = 4
D_ATTN = 1024
KV_DIM = N_KV_HEADS * HEAD_DIM
QKV_DIM = D_ATTN + 2 * KV_DIM
ROPE_THETA = 10000.0
N_EXPERTS = 16
EC_CAPACITY = 2

LANES = 128
SUBLANES = 8
VMEM_LIMIT_BYTES = 56 * 1024 * 1024


def _cparams(*sem):
    return pltpu.CompilerParams(dimension_semantics=sem, vmem_limit_bytes=VMEM_LIMIT_BYTES)


def _sigmoid(x):
    return 1.0 / (1.0 + jnp.exp(-x))


def _silu(x):
    return x * _sigmoid(x)


def _split2(x):
    hi = x.astype(BF16)
    return hi, (x - hi.astype(F32)).astype(BF16)


def _ada_kernel(c_ref, w_ref, b_ref, o_ref):
    a = _silu(c_ref[...])
    o_ref[...] = jnp.dot(a, w_ref[...], precision=HIGHEST, preferred_element_type=F32) + b_ref[...]


def ada_modulation(c_all, ada_w, ada_b):
    depth, d, n = ada_w.shape
    m = c_all.shape[0]
    tn = 1536
    return pl.pallas_call(
        _ada_kernel,
        grid=(depth, n // tn),
        in_specs=[pl.BlockSpec((m, d), lambda l, j: (0, 0)),
                  pl.BlockSpec((None, d, tn), lambda l, j: (l, 0, j)),
                  pl.BlockSpec((None, 1, tn), lambda l, j: (l, 0, j))],
        out_specs=pl.BlockSpec((None, m, tn), lambda l, j: (l, 0, j)),
        out_shape=jax.ShapeDtypeStruct((depth, m, n), F32),
        compiler_params=_cparams("parallel", "parallel"),
        name="ada_modulation",
    )(c_all, ada_w, ada_b.reshape(depth, 1, n))


def _norm_mod(x_ref, wm_ref, sh_ref):
    x = x_ref[...]
    ms = jnp.mean(x * x, axis=-1, keepdims=True)
    return (x * lax.rsqrt(ms + EPS) * wm_ref[...] + sh_ref[...]).astype(BF16)


def _norm_mod_mm_kernel(x_ref, wm_ref, sh_ref, w_ref, o_ref, a_scr):
    @pl.when(pl.program_id(1) == 0)
    def _():
        a_scr[...] = _norm_mod(x_ref, wm_ref, sh_ref)

    o_ref[...] = jnp.dot(a_scr[...], w_ref[...], preferred_element_type=F32).astype(BF16)


def _norm_mod_mm2_kernel(x_ref, wm_ref, sh_ref, w_ref, wf_ref, o_ref, of_ref, a_scr):
    @pl.when(pl.program_id(1) == 0)
    def _():
        a_scr[...] = _norm_mod(x_ref, wm_ref, sh_ref)
        of_ref[...] = jnp.dot(a_scr[...], wf_ref[...], preferred_element_type=F32)

    o_ref[...] = jnp.dot(a_scr[...], w_ref[...], preferred_element_type=F32).astype(BF16)


def norm_mod_mm(x2d, wm, sh, w, rows_per_group, tm, tn, name, w_f32out=None):
    m, k = x2d.shape
    n = w.shape[1]
    tiles_per_group = rows_per_group // tm
    in_specs = [pl.BlockSpec((tm, k), lambda i, j: (i, 0)),
                pl.BlockSpec((None, 1, k), lambda i, j: (i // tiles_per_group, 0, 0)),
                pl.BlockSpec((None, 1, k), lambda i, j: (i // tiles_per_group, 0, 0)),
                pl.BlockSpec((k, tn), lambda i, j: (0, j))]
    out_specs = [pl.BlockSpec((tm, tn), lambda i, j: (i, j))]
    out_shape = [jax.ShapeDtypeStruct((m, n), BF16)]
    args = [x2d, wm, sh, w]
    body = _norm_mod_mm_kernel
    if w_f32out is not None:
        nf = w_f32out.shape[1]
        in_specs.append(pl.BlockSpec((k, nf), lambda i, j: (0, 0)))
        out_specs.append(pl.BlockSpec((tm, nf), lambda i, j: (i, 0)))
        out_shape.append(jax.ShapeDtypeStruct((m, nf), F32))
        args.append(w_f32out)
        body = _norm_mod_mm2_kernel
    return pl.pallas_call(
        body,
        grid=(m // tm, n // tn),
        in_specs=in_specs, out_specs=out_specs, out_shape=out_shape,
        scratch_shapes=[pltpu.VMEM((tm, k), BF16)],
        compiler_params=_cparams("parallel", "arbitrary"),
        name=name,
    )(*args)


def _res_mm_kernel(a_ref, w_ref, res_ref, gate_ref, o_ref):
    acc = jnp.dot(a_ref[...], w_ref[...], preferred_element_type=F32)
    o_ref[...] = res_ref[...] + gate_ref[...] * acc


def res_mm(a, w, res, gate, rows_per_group, tm, name):
    m, k = a.shape
    n = w.shape[1]
    tiles_per_group = rows_per_group // tm
    return pl.pallas_call(
        _res_mm_kernel,
        grid=(m // tm,),
        in_specs=[pl.BlockSpec((tm, k), lambda i: (i, 0)),
                  pl.BlockSpec((k, n), lambda i: (0, 0)),
                  pl.BlockSpec((tm, n), lambda i: (i, 0)),
                  pl.BlockSpec((None, 1, n), lambda i: (i // tiles_per_group, 0, 0))],
        out_specs=pl.BlockSpec((tm, n), lambda i: (i, 0)),
        out_shape=jax.ShapeDtypeStruct((m, n), F32),
        compiler_params=_cparams("parallel"),
        name=name,
    )(a, w, res, gate)


def _ssd_out_kernel(y_ref, z_ref, nw_ref, w_ref, res_ref, gate_ref, o_ref):
    v = y_ref[...].astype(F32) * _silu(z_ref[...].astype(F32))
    ms = jnp.mean(v * v, axis=-1, keepdims=True)
    a = (v * lax.rsqrt(ms + EPS) * nw_ref[...]).astype(BF16)
    acc = jnp.dot(a, w_ref[...], preferred_element_type=F32)
    o_ref[...] = res_ref[...] + gate_ref[...] * acc


def ssd_out_mm(y, proj, norm_w, w, res, gate, rows_per_group, tm, name):
    m, k = y.shape
    n = w.shape[1]
    tiles_per_group = rows_per_group // tm
    return pl.pallas_call(
        _ssd_out_kernel,
        grid=(m // tm,),
        in_specs=[pl.BlockSpec((tm, k), lambda i: (i, 0)),
                  pl.BlockSpec((tm, k), lambda i: (i, 0)),
                  pl.BlockSpec((1, k), lambda i: (0, 0)),
                  pl.BlockSpec((k, n), lambda i: (0, 0)),
                  pl.BlockSpec((tm, n), lambda i: (i, 0)),
                  pl.BlockSpec((None, 1, n), lambda i: (i // tiles_per_group, 0, 0))],
        out_specs=pl.BlockSpec((tm, n), lambda i: (i, 0)),
        out_shape=jax.ShapeDtypeStruct((m, n), F32),
        compiler_params=_cparams("parallel"),
        name=name,
    )(y, proj, norm_w, w, res, gate)


def _ssd_prep_kernel(dtr_ref, bias_ref, a_ref, scol_ref, srow_ref, col_ref, row_ref):
    seq = dtr_ref.shape[0]
    x = dtr_ref[...] + bias_ref[...]
    dt = jnp.maximum(x, 0.0) + jnp.log1p(jnp.exp(-jnp.abs(x)))
    a = dt * a_ref[...]
    t_i = lax.broadcasted_iota(jnp.int32, (SSD_CHUNK, SSD_CHUNK), 0)
    s_i = lax.broadcasted_iota(jnp.int32, (SSD_CHUNK, SSD_CHUNK), 1)
    tri_lo = (s_i <= t_i).astype(F32)
    tri_up = (s_i >= t_i).astype(F32)
    fwd_lane = lax.broadcasted_iota(jnp.int32, (SSD_CHUNK, LANES), 1) < SSD_HEADS
    chunks = []
    for c in range(seq // SSD_CHUNK):
        ac = a[c * SSD_CHUNK:(c + 1) * SSD_CHUNK]
        cf = jnp.dot(tri_lo, ac, precision=HIGHEST, preferred_element_type=F32)
        cb = jnp.dot(tri_up, ac, precision=HIGHEST, preferred_element_type=F32)
        chunks.append(jnp.where(fwd_lane, cf, cb))
    cum = jnp.concatenate(chunks, axis=0)
    both = jnp.concatenate([dt, cum], axis=1)
    hi = both.astype(BF16)
    rest = both - hi.astype(F32)
    mid = rest.astype(BF16)
    lo = (rest - mid.astype(F32)).astype(BF16)
    scol = scol_ref[...]
    col_ref[...] = (jnp.dot(hi, scol, preferred_element_type=F32)
                    + jnp.dot(mid, scol, preferred_element_type=F32)
                    + jnp.dot(lo, scol, preferred_element_type=F32))
    row_ref[...] = lax.dot_general(srow_ref[...], cum, (((1,), (1,)), ((), ())),
                                   precision=HIGHEST, preferred_element_type=F32)


def _ssd_select_tables():
    scol = np.zeros((2 * LANES, SSD_GROUPS * LANES), np.float32)
    srow = np.zeros((SSD_GROUPS * SUBLANES, LANES), np.float32)
    for g in range(SSD_GROUPS):
        for d in range(2):
            for r in range(SSD_HPG):
                lane = d * SSD_HEADS + SSD_HPG * g + r
                scol[lane, g * LANES + d * SSD_HPG + r] = 1.0
                scol[LANES + lane, g * LANES + 2 * SSD_HPG + d * SSD_HPG + r] = 1.0
                srow[g * SUBLANES + d * SSD_HPG + r, lane] = 1.0
    return jnp.asarray(scol, dtype=BF16), jnp.asarray(srow)


def ssd_prep(dt_raw3, dt_bias, a_neg, name):
    b, seq, _ = dt_raw3.shape
    scol, srow = _ssd_select_tables()
    return pl.pallas_call(
        _ssd_prep_kernel,
        grid=(b,),
        in_specs=[pl.BlockSpec((None, seq, LANES), lambda i: (i, 0, 0)),
                  pl.BlockSpec((1, LANES), lambda i: (0, 0)),
                  pl.BlockSpec((1, LANES), lambda i: (0, 0)),
                  pl.BlockSpec(scol.shape, lambda i: (0, 0)),
                  pl.BlockSpec(srow.shape, lambda i: (0, 0))],
        out_specs=[pl.BlockSpec((None, seq, SSD_GROUPS * LANES), lambda i: (i, 0, 0)),
                   pl.BlockSpec((None, SSD_GROUPS * SUBLANES, seq), lambda i: (i, 0, 0))],
        out_shape=[jax.ShapeDtypeStruct((b, seq, SSD_GROUPS * LANES), F32),
                   jax.ShapeDtypeStruct((b, SSD_GROUPS * SUBLANES, seq), F32)],
        compiler_params=_cparams("parallel"),
        name=name,
    )(dt_raw3, dt_bias, a_neg, scol, srow)


CONV_HALO = 2 * SUBLANES
CONV_TAPS_OFF_CENTRE = (0, 1, 3, 4)
SSD_UNROLL = 4


def _ssd_scan_kernel(xc_ref, bc_ref, cc_ref, xl_ref, bl_ref, cl_ref,
                     cwx_ref, cwb_ref, cwc_ref, cbx_ref, cbb_ref, cbc_ref,
                     colc_ref, rowc_ref, coll_ref, rowl_ref, dsk_ref,
                     yc_ref, yl_ref,
                     pad_scr, xs_scr, bt_scr, cs_scr, y_scr, xdt_scr, xdw_scr, ecum_scr, seg_scr, ed_scr,
                     hf_scr, hb_scr):
    q = SSD_CHUNK
    width = SSD_HPG * SSD_HEADDIM
    win_rows = q + 2 * CONV_HALO
    dsk = dsk_ref[...]

    r_i = lax.broadcasted_iota(jnp.int32, (len(CONV_TAPS_OFF_CENTRE) * q, win_rows), 0)
    j_i = lax.broadcasted_iota(jnp.int32, (len(CONV_TAPS_OFF_CENTRE) * q, win_rows), 1)
    tap_of_row = jnp.zeros_like(r_i)
    for n, k in enumerate(CONV_TAPS_OFF_CENTRE):
        tap_of_row = jnp.where(r_i // q == n, k, tap_of_row)
    shift_mat = jnp.where(j_i == r_i % q + CONV_HALO - CONV_W // 2 + tap_of_row, 1.0, 0.0).astype(BF16)

    def conv_silu(parts, w, bias, seq, store):
        zeros = jnp.zeros((CONV_HALO, width), BF16)
        pad_scr[0:CONV_HALO, :] = zeros
        pad_scr[CONV_HALO + seq:2 * CONV_HALO + seq, :] = zeros

        def copy_body(c, carry):
            base = pl.multiple_of(c * q, q)
            dst = pl.multiple_of(base + CONV_HALO, CONV_HALO)
            for raw_ref, lane0 in parts:
                pad_scr[pl.ds(dst, q), lane0:lane0 + raw_ref.shape[1]] = raw_ref[pl.ds(base, q), :]
            return carry

        lax.fori_loop(0, seq // q, copy_body, 0)

        def conv_body(c, carry):
            base = pl.multiple_of(c * q, q)
            win = pad_scr[pl.ds(base, win_rows), :]
            shifted = jnp.dot(shift_mat, win, preferred_element_type=F32)
            acc = bias + win[CONV_HALO:CONV_HALO + q].astype(F32) * w[CONV_W // 2:CONV_W // 2 + 1]
            for n, k in enumerate(CONV_TAPS_OFF_CENTRE):
                acc = acc + shifted[n * q:(n + 1) * q] * w[k:k + 1]
            store(base, _silu(acc))
            return carry

        lax.fori_loop(0, seq // q, conv_body, 0, unroll=min(SSD_UNROLL, seq // q))

    def x_store(base, v):
        xs_scr[pl.ds(base, q), :] = v
        y_scr[pl.ds(base, q), :] = dsk * v

    def bc_store(base, v):
        bt_scr[:, pl.ds(base, q)] = v[:, 0:D_STATE].T.astype(BF16)
        cs_scr[pl.ds(base, q), :] = v[:, D_STATE:2 * D_STATE].astype(BF16)

    w_bc = jnp.concatenate([cwb_ref[...], cwc_ref[...]], axis=1)
    b_bc = jnp.concatenate([cbb_ref[...], cbc_ref[...]], axis=1)

    t_i = lax.broadcasted_iota(jnp.int32, (q, q), 0)
    s_i = lax.broadcasted_iota(jnp.int32, (q, q), 1)
    head_of_lane = lax.broadcasted_iota(jnp.int32, (q, width), 1) // SSD_HEADDIM

    def per_head(parts):
        out = parts[SSD_HPG - 1]
        for r in range(SSD_HPG - 2, -1, -1):
            out = jnp.where(head_of_lane == r, parts[r], out)
        return out

    src_lane = lax.broadcasted_iota(jnp.int32, (LANES, 2 * width), 0)
    dst_head = lax.broadcasted_iota(jnp.int32, (LANES, 2 * width), 1) // SSD_HEADDIM
    dt_spread = jnp.where(src_lane == dst_head, 1.0, 0.0).astype(BF16)

    first_head_lanes = lax.broadcasted_iota(jnp.int32, (q, q), 1) < SSD_HEADDIM

    def per_head_lanes(parts):
        return jnp.concatenate([jnp.where(first_head_lanes, parts[0], parts[1]),
                                jnp.where(first_head_lanes, parts[2], parts[3])], axis=1)

    def run_seq(x_ref, b_ref, c_ref, col_ref, row_ref, y_ref):
        seq = x_ref.shape[0]
        nchunks = seq // q
        conv_silu([(x_ref, 0)], cwx_ref[...], cbx_ref[...], seq, x_store)
        conv_silu([(b_ref, 0), (c_ref, D_STATE)], w_bc, b_bc, seq, bc_store)

        def decay_terms(c, d, cp, dt_e):
            mask = (s_i <= t_i) if d == 0 else (s_i >= t_i)
            edge = q - 1 if d == 0 else 0
            base = pl.multiple_of(c * q, q)
            rp = row_ref[:, pl.ds(base, q)]
            cum_b = [jnp.broadcast_to(cp[:, 2 * SSD_HPG + SSD_HPG * d + r:2 * SSD_HPG + SSD_HPG * d + r + 1],
                                      (q, q)) for r in range(SSD_HPG)]
            cum_e = per_head_lanes(cum_b)
            xdt = xs_scr[pl.ds(base, q), :] * dt_e
            cum_edge = cum_e[edge:edge + 1, :]
            xdt_scr[d, pl.ds(base, q), :] = xdt.astype(BF16)
            xdw_scr[d, pl.ds(base, q), :] = (xdt * jnp.exp(cum_edge - cum_e)).astype(BF16)
            ecum_scr[d, pl.ds(base, q), :] = jnp.exp(cum_e)
            ed_scr[d, pl.ds(pl.multiple_of(c * SUBLANES, SUBLANES), SUBLANES), :] = jnp.broadcast_to(
                jnp.exp(cum_edge), (SUBLANES, width))
            for r in range(SSD_HPG):
                row = rp[SSD_HPG * d + r:SSD_HPG * d + r + 1, :]
                seg = jnp.exp(jnp.where(mask, cum_b[r] - row, -1e30))
                seg_scr[SSD_HPG * d + r, pl.ds(base, q), :] = seg.astype(BF16)

        def prep_body(c, carry):
            cp = col_ref[pl.ds(pl.multiple_of(c * q, q), q), :]
            cp_hi, cp_lo = _split2(cp)
            dt_all = (jnp.dot(cp_hi, dt_spread, preferred_element_type=F32)
                      + jnp.dot(cp_lo, dt_spread, preferred_element_type=F32))
            decay_terms(c, 0, cp, dt_all[:, 0:width])
            decay_terms(c, 1, cp, dt_all[:, width:2 * width])
            return carry

        lax.fori_loop(0, nchunks, prep_body, 0, unroll=min(SSD_UNROLL, nchunks))

        def chunk(c, d, h_scr):
            base = pl.multiple_of(c * q, q)
            bq_t = bt_scr[:, pl.ds(base, q)]
            cq = cs_scr[pl.ds(base, q), :]
            cb = jnp.dot(cq, bq_t, preferred_element_type=F32).astype(BF16)
            ms = [cb * seg_scr[SSD_HPG * d + r, pl.ds(base, q), :] for r in range(SSD_HPG)]
            stacked = jnp.dot(jnp.concatenate(ms, axis=0), xdt_scr[d, pl.ds(base, q), :],
                              preferred_element_type=F32)
            y = per_head([stacked[r * q:(r + 1) * q] for r in range(SSD_HPG)])
            h_t = h_scr[...]
            y = y + jnp.dot(cq, h_t.astype(BF16), preferred_element_type=F32) * ecum_scr[d, pl.ds(base, q), :]
            edge_decay = ed_scr[d, pl.ds(pl.multiple_of(c * SUBLANES, SUBLANES), SUBLANES), :][0:1]
            h_scr[...] = h_t * edge_decay + jnp.dot(bq_t, xdw_scr[d, pl.ds(base, q), :],
                                                    preferred_element_type=F32)
            y_scr[pl.ds(base, q), :] += y

        def scan_body(i, carry):
            chunk(i, 0, hf_scr)
            chunk(nchunks - 1 - i, 1, hb_scr)
            return carry

        lax.fori_loop(0, nchunks, scan_body, 0, unroll=min(SSD_UNROLL, nchunks))

        def out_body(c, carry):
            base = pl.multiple_of(c * q, q)
            y_ref[pl.ds(base, q), :] = y_scr[pl.ds(base, q), :].astype(BF16)
            return carry

        lax.fori_loop(0, nchunks, out_body, 0)

    hf_scr[...] = jnp.zeros(hf_scr.shape, F32)
    hb_scr[...] = jnp.zeros(hb_scr.shape, F32)
    run_seq(xc_ref, bc_ref, cc_ref, colc_ref, rowc_ref, yc_ref)
    run_seq(xl_ref, bl_ref, cl_ref, coll_ref, rowl_ref, yl_ref)


def ssd_scan(proj_c, proj_l, conv_w, conv_b, col_c, row_c, col_l, row_l, dskip_e):
    b, lc, _ = proj_c.shape
    ll = proj_l.shape[1]
    width = SSD_HPG * SSD_HEADDIM
    x0 = D_INNER // width
    b0 = (2 * D_INNER) // D_STATE
    c0 = b0 + SSD_GROUPS
    wx0, wb0, wc0 = 0, D_INNER // D_STATE, D_INNER // D_STATE + SSD_GROUPS

    def seq_specs(seq):
        return [pl.BlockSpec((None, seq, width), lambda i, g: (i, 0, x0 + g)),
                pl.BlockSpec((None, seq, D_STATE), lambda i, g: (i, 0, b0 + g)),
                pl.BlockSpec((None, seq, D_STATE), lambda i, g: (i, 0, c0 + g))]

    def par_specs(rows):
        return [pl.BlockSpec((rows, width), lambda i, g: (0, wx0 + g)),
                pl.BlockSpec((rows, D_STATE), lambda i, g: (0, wb0 + g)),
                pl.BlockSpec((rows, D_STATE), lambda i, g: (0, wc0 + g))]

    def pack_specs(seq):
        return [pl.BlockSpec((None, seq, LANES), lambda i, g: (i, 0, g)),
                pl.BlockSpec((None, SUBLANES, seq), lambda i, g: (i, g, 0))]

    return pl.pallas_call(
        _ssd_scan_kernel,
        grid=(b, SSD_GROUPS),
        in_specs=(seq_specs(lc) + seq_specs(ll) + par_specs(CONV_W) + par_specs(1)
                  + pack_specs(lc) + pack_specs(ll)
                  + [pl.BlockSpec((None, 1, width), lambda i, g: (g, 0, 0))]),
        out_specs=[pl.BlockSpec((None, lc, width), lambda i, g: (i, 0, g)),
                   pl.BlockSpec((None, ll, width), lambda i, g: (i, 0, g))],
        out_shape=[jax.ShapeDtypeStruct((b, lc, D_INNER), BF16),
                   jax.ShapeDtypeStruct((b, ll, D_INNER), BF16)],
        scratch_shapes=[pltpu.VMEM((ll + 2 * CONV_HALO, width), BF16),
                        pltpu.VMEM((ll, width), F32),
                        pltpu.VMEM((D_STATE, ll), BF16),
                        pltpu.VMEM((ll, D_STATE), BF16),
                        pltpu.VMEM((ll, width), F32),
                        pltpu.VMEM((2, ll, width), BF16),
                        pltpu.VMEM((2, ll, width), BF16),
                        pltpu.VMEM((2, ll, width), F32),
                        pltpu.VMEM((2 * SSD_HPG, ll, SSD_CHUNK), BF16),
                        pltpu.VMEM((2, ll // SSD_CHUNK * SUBLANES, width), F32),
                        pltpu.VMEM((D_STATE, width), F32), pltpu.VMEM((D_STATE, width), F32)],
        compiler_params=_cparams("parallel", "parallel"),
        name="ssd_scan",
    )(proj_c, proj_c, proj_c, proj_l, proj_l, proj_l,
      conv_w, conv_w, conv_w, conv_b, conv_b, conv_b,
      col_c, row_c, col_l, row_l, dskip_e)


def _qk_prep_kernel(qkv_ref, gmat_ref, qw_ref, kw_ref, rope_c_ref, rope_a_ref, rope_b_ref, rep_ref,
                    q_ref, k_ref, v_ref, *, use_rope):
    blk = Q_PER_KV * HEAD_DIM
    gmat = gmat_ref[...]

    def norm_rope(x, w):
        sq_hi, sq_lo = _split2(x * x)
        ms = (jnp.dot(sq_hi, gmat, preferred_element_type=F32)
              + jnp.dot(sq_lo, gmat, preferred_element_type=F32))
        xn = x * lax.rsqrt(ms + EPS) * w
        if use_rope:
            half = HEAD_DIM // 2
            xn = (xn * rope_c_ref[...] + pltpu.roll(xn, blk - half, 1) * rope_a_ref[...]
                  + pltpu.roll(xn, half, 1) * rope_b_ref[...])
        return xn

    scale = math.log2(math.e) / math.sqrt(HEAD_DIM)
    for j in range(D_ATTN // blk):
        xq = norm_rope(qkv_ref[:, j * blk:(j + 1) * blk].astype(F32), qw_ref[...])
        q_ref[:, j * blk:(j + 1) * blk] = (xq * scale).astype(BF16)
    xk = norm_rope(qkv_ref[:, D_ATTN:D_ATTN + KV_DIM].astype(F32), kw_ref[...]).astype(BF16)
    xv = qkv_ref[:, D_ATTN + KV_DIM:D_ATTN + 2 * KV_DIM]
    for kv in range(N_KV_HEADS):
        rep = rep_ref[kv]
        k_ref[:, kv * blk:(kv + 1) * blk] = jnp.dot(xk, rep, preferred_element_type=F32).astype(BF16)
        v_ref[:, kv * blk:(kv + 1) * blk] = jnp.dot(xv, rep, preferred_element_type=F32).astype(BF16)


def _attn_tables(seq):
    rows = seq // GRID_W
    row = np.repeat(np.arange(rows), GRID_W).astype(np.float32)
    col = np.tile(np.arange(GRID_W), rows).astype(np.float32)
    axis_dim = HEAD_DIM // 2
    inv_freq = jnp.asarray(ROPE_THETA, F32) ** (-jnp.arange(0, axis_dim, 2, dtype=F32) / axis_dim)
    ang = jnp.concatenate([jnp.asarray(row)[:, None] * inv_freq, jnp.asarray(col)[:, None] * inv_freq], axis=-1)
    cos, sin = jnp.cos(ang), jnp.sin(ang)
    zero = jnp.zeros_like(sin)
    reps = Q_PER_KV
    rope_c = jnp.tile(jnp.concatenate([cos, cos], axis=-1), (1, reps))
    rope_a = jnp.tile(jnp.concatenate([-sin, zero], axis=-1), (1, reps))
    rope_b = jnp.tile(jnp.concatenate([zero, sin], axis=-1), (1, reps))
    return rope_c, rope_a, rope_b


def _attn_consts():
    blk = Q_PER_KV * HEAD_DIM
    head = np.arange(blk) // HEAD_DIM
    gmat = (head[:, None] == head[None, :]).astype(np.float32) / HEAD_DIM
    rep = np.zeros((N_KV_HEADS, KV_DIM, blk), np.float32)
    for kv in range(N_KV_HEADS):
        for j in range(blk):
            rep[kv, kv * HEAD_DIM + j % HEAD_DIM, j] = 1.0
    return jnp.asarray(gmat, dtype=BF16), jnp.asarray(rep, dtype=BF16)


def qk_prep(qkv, q_norm_w, k_norm_w, tables, seq, use_rope, tl, name):
    m = qkv.shape[0]
    blk = Q_PER_KV * HEAD_DIM
    gmat, rep = _attn_consts()
    qw = jnp.tile(q_norm_w.reshape(1, HEAD_DIM), (1, Q_PER_KV))
    kw = jnp.tile(k_norm_w.reshape(1, HEAD_DIM), (1, N_KV_HEADS))
    tiles_per_seq = seq // tl
    tab_spec = pl.BlockSpec((tl, blk), lambda i: (i % tiles_per_seq, 0))
    out_w = N_KV_HEADS * blk
    return pl.pallas_call(
        functools.partial(_qk_prep_kernel, use_rope=use_rope),
        grid=(m // tl,),
        in_specs=[pl.BlockSpec((tl, QKV_DIM), lambda i: (i, 0)),
                  pl.BlockSpec((blk, blk), lambda i: (0, 0)),
                  pl.BlockSpec((1, blk), lambda i: (0, 0)),
                  pl.BlockSpec((1, blk), lambda i: (0, 0)),
                  tab_spec, tab_spec, tab_spec,
                  pl.BlockSpec(rep.shape, lambda i: (0, 0, 0))],
        out_specs=[pl.BlockSpec((tl, D_ATTN), lambda i: (i, 0)),
                   pl.BlockSpec((tl, out_w), lambda i: (i, 0)),
                   pl.BlockSpec((tl, out_w), lambda i: (i, 0))],
        out_shape=[jax.ShapeDtypeStruct((m, D_ATTN), BF16),
                   jax.ShapeDtypeStruct((m, out_w), BF16),
                   jax.ShapeDtypeStruct((m, out_w), BF16)],
        compiler_params=_cparams("parallel"),
        name=name,
    )(qkv, gmat, qw, kw, *tables, rep)


def _attn_kernel(q_ref, kl_ref, vl_ref, kc_ref, vc_ref, o_ref):
    tq = q_ref.shape[0]
    blk = Q_PER_KV * HEAD_DIM
    head_of_lane = lax.broadcasted_iota(jnp.int32, (tq, blk), 1) // HEAD_DIM
    nt = (((1,), (1,)), ((), ()))
    for kv in range(N_KV_HEADS):
        cols = slice(kv * blk, (kv + 1) * blk)
        q4 = q_ref[:, cols]
        zero = jnp.zeros_like(q4)
        qs = jnp.concatenate([jnp.where(head_of_lane == g, q4, zero) for g in range(Q_PER_KV)], axis=0)
        half = kl_ref.shape[0] // 2
        s_a = lax.dot_general(qs, kl_ref[0:half, cols], nt, preferred_element_type=F32)
        s_b = lax.dot_general(qs, kl_ref[half:2 * half, cols], nt, preferred_element_type=F32)
        s_c = lax.dot_general(qs, kc_ref[:, cols], nt, preferred_element_type=F32)
        m = jnp.maximum(jnp.maximum(jnp.max(s_a, axis=-1, keepdims=True), jnp.max(s_b, axis=-1, keepdims=True)),
                        jnp.max(s_c, axis=-1, keepdims=True))
        p_a = jnp.exp2(s_a - m)
        p_b = jnp.exp2(s_b - m)
        p_c = jnp.exp2(s_c - m)
        denom = (jnp.sum(p_a, axis=-1, keepdims=True) + jnp.sum(p_b, axis=-1, keepdims=True)
                 + jnp.sum(p_c, axis=-1, keepdims=True))
        rows = Q_PER_KV * tq // 2
        o_parts = []
        for h in range(2):
            rs = slice(h * rows, (h + 1) * rows)
            o_parts.append(
                jnp.dot(p_a[rs].astype(BF16), vl_ref[0:half, cols], preferred_element_type=F32)
                + jnp.dot(p_b[rs].astype(BF16), vl_ref[half:2 * half, cols], preferred_element_type=F32)
                + jnp.dot(p_c[rs].astype(BF16), vc_ref[:, cols], preferred_element_type=F32))
        o = jnp.concatenate(o_parts, axis=0) / denom
        out = o[(Q_PER_KV - 1) * tq:Q_PER_KV * tq]
        for g in range(Q_PER_KV - 2, -1, -1):
            out = jnp.where(head_of_lane == g, o[g * tq:(g + 1) * tq], out)
        o_ref[:, cols] = out.astype(BF16)


def attention(q, k_lat, v_lat, k_ctx, v_ctx, batch, seq, ctx_len, tq):
    width = q.shape[1]
    nq = seq // tq
    return pl.pallas_call(
        _attn_kernel,
        grid=(batch, nq),
        in_specs=[pl.BlockSpec((tq, width), lambda b, i: (b * nq + i, 0)),
                  pl.BlockSpec((seq, width), lambda b, i: (b, 0)),
                  pl.BlockSpec((seq, width), lambda b, i: (b, 0)),
                  pl.BlockSpec((ctx_len, width), lambda b, i: (b, 0)),
                  pl.BlockSpec((ctx_len, width), lambda b, i: (b, 0))],
        out_specs=pl.BlockSpec((tq, width), lambda b, i: (b * nq + i, 0)),
        out_shape=jax.ShapeDtypeStruct((batch * seq, width), BF16),
        compiler_params=_cparams("parallel", "parallel"),
        name="attention",
    )(q, k_lat, v_lat, k_ctx, v_ctx)


def _router_kernel(x_ref, wm_ref, sh_ref, rw_ref, hn_ref, pos_ref, gate_ref, *, cap):
    tokens = x_ref.shape[0]
    x = x_ref[...]
    ms = jnp.mean(x * x, axis=-1, keepdims=True)
    hn = x * lax.rsqrt(ms + EPS) * wm_ref[...] + sh_ref[...]
    hn_hi, hn_lo = _split2(hn)
    hn_ref[...] = hn_hi
    rw_hi, rw_lo = _split2(rw_ref[...])
    nt = (((1,), (1,)), ((), ()))
    logits = (lax.dot_general(rw_hi, hn_hi, nt, preferred_element_type=F32)
              + lax.dot_general(rw_hi, hn_lo, nt, preferred_element_type=F32)
              + lax.dot_general(rw_lo, hn_hi, nt, preferred_element_type=F32))
    e = jnp.exp(logits - jnp.max(logits, axis=0, keepdims=True))
    aff = e / jnp.sum(e, axis=0, keepdims=True)
    gate_ref[...] = aff
    bits = pltpu.bitcast(aff, jnp.int32)

    def count(flags):
        return jnp.sum(jnp.where(flags, 1.0, 0.0), axis=1, keepdims=True)

    def search(i, thr):
        cand = thr | jnp.left_shift(jnp.int32(1), 30 - i)
        return jnp.where(count(bits >= cand) >= cap, cand, thr)

    thr = lax.fori_loop(0, 31, search, jnp.zeros((N_EXPERTS, 1), jnp.int32))
    above = bits > thr
    tied = bits == thr
    need = cap - count(above)

    t_i = lax.broadcasted_iota(jnp.int32, (LANES, LANES), 0)
    s_i = lax.broadcasted_iota(jnp.int32, (LANES, LANES), 1)
    tri = (t_i <= s_i).astype(BF16)

    def lane_cumsum(flags):
        carry = jnp.zeros((N_EXPERTS, 1), F32)
        parts = []
        for c in range(tokens // LANES):
            part = jnp.dot(flags[:, c * LANES:(c + 1) * LANES].astype(BF16), tri,
                           preferred_element_type=F32) + carry
            parts.append(part)
            carry = part[:, LANES - 1:LANES]
        return jnp.concatenate(parts, axis=1)

    tie_rank = lane_cumsum(jnp.where(tied, 1.0, 0.0))
    sel = above | (tied & (tie_rank <= need))
    slot = lane_cumsum(jnp.where(sel, 1.0, 0.0)) - 1.0
    pos_ref[...] = jnp.where(sel, slot, -1.0).astype(jnp.int32)


def moe_router(x3, wm, sh, router_wt, cap, name):
    b, tokens, d = x3.shape
    per_batch = wm.shape[0] > 1
    mod_spec = pl.BlockSpec((None, 1, d), (lambda i: (i, 0, 0)) if per_batch else (lambda i: (0, 0, 0)))
    return pl.pallas_call(
        functools.partial(_router_kernel, cap=cap),
        grid=(b,),
        in_specs=[pl.BlockSpec((None, tokens, d), lambda i: (i, 0, 0)),
                  mod_spec, mod_spec,
                  pl.BlockSpec((N_EXPERTS, d), lambda i: (0, 0))],
        out_specs=[pl.BlockSpec((None, tokens, d), lambda i: (i, 0, 0)),
                   pl.BlockSpec((None, N_EXPERTS, tokens), lambda i: (i, 0, 0)),
                   pl.BlockSpec((None, N_EXPERTS, tokens), lambda i: (i, 0, 0))],
        out_shape=[jax.ShapeDtypeStruct((b, tokens, d), BF16),
                   jax.ShapeDtypeStruct((b, N_EXPERTS, tokens), jnp.int32),
                   jax.ShapeDtypeStruct((b, N_EXPERTS, tokens), F32)],
        compiler_params=_cparams("parallel"),
        name=name,
    )(x3, wm, sh, router_wt)


def _expert_kernel(hn_ref, pos_ref, gate_ref, w1_ref, w3_ref, w2_ref, g2_ref, x_ref, fw_ref, o_ref,
                   *, cap, final_norm):
    nb, tokens, _ = hn_ref.shape
    e = pl.program_id(1)

    @pl.when(e == 0)
    def _():
        o_ref[...] = x_ref[...]

    slot_i = lax.broadcasted_iota(jnp.int32, (cap, tokens), 0)
    onehots, gathered, gates = [], [], []
    for n in range(nb):
        hit = slot_i == pos_ref[n]
        onehot = jnp.where(hit, 1.0, 0.0).astype(BF16)
        onehots.append(onehot)
        gates.append(jnp.sum(jnp.where(hit, gate_ref[n], 0.0), axis=1, keepdims=True))
        gathered.append(jnp.dot(onehot, hn_ref[n], preferred_element_type=F32).astype(BF16))
    xg = jnp.concatenate(gathered, axis=0) if nb > 1 else gathered[0]
    a = jnp.dot(xg, w1_ref[...], preferred_element_type=F32)
    g = jnp.dot(xg, w3_ref[...], preferred_element_type=F32)
    y = jnp.dot((_silu(a) * g).astype(BF16), w2_ref[...], preferred_element_type=F32)
    for n in range(nb):
        yn = (y[n * cap:(n + 1) * cap] * gates[n] * g2_ref[n]).astype(BF16)
        o_ref[n] += lax.dot_general(onehots[n], yn, (((0,), (0,)), ((), ())),
                                    preferred_element_type=F32)

    if final_norm:
        @pl.when(e == pl.num_programs(1) - 1)
        def _():
            for n in range(nb):
                v = o_ref[n]
                ms = jnp.mean(v * v, axis=-1, keepdims=True)
                o_ref[n] = v * lax.rsqrt(ms + EPS) * fw_ref[...]


def moe_experts(hn, pos, gate, w1, w3, w2, g2, x3, final_w, cap, nb, final_norm, name):
    b, tokens, d = hn.shape
    g2 = jnp.broadcast_to(g2, (b, 1, d))
    g2_spec = pl.BlockSpec((nb, 1, d), lambda i, e: (i, 0, 0))
    pos4 = pos.reshape(b, N_EXPERTS, 1, tokens)
    gate4 = gate.reshape(b, N_EXPERTS, 1, tokens)
    sel_spec = pl.BlockSpec((nb, None, 1, tokens), lambda i, e: (i, e, 0, 0))
    w_spec = pl.BlockSpec((None, d, d), lambda i, e: (e, 0, 0))
    tok_spec = pl.BlockSpec((nb, tokens, d), lambda i, e: (i, 0, 0))
    acc_spec = pl.BlockSpec((nb, tokens, d), lambda i, e: (i, 0, 0), pipeline_mode=pl.Buffered(1))
    return pl.pallas_call(
        functools.partial(_expert_kernel, cap=cap, final_norm=final_norm),
        grid=(b // nb, N_EXPERTS),
        in_specs=[tok_spec, sel_spec, sel_spec, w_spec, w_spec, w_spec, g2_spec, tok_spec,
                  pl.BlockSpec((1, d), lambda i, e: (0, 0))],
        out_specs=acc_spec,
        out_shape=jax.ShapeDtypeStruct((b, tokens, d), F32),
        compiler_params=_cparams("parallel", "arbitrary"),
        name=name,
    )(hn, pos4, gate4, w1, w3, w2, g2, x3, final_w)


def moe_block(x3, wm, sh, g2, router_wt, w1, w3, w2, final_w, nb, final_norm, name):
    tokens = x3.shape[1]
    cap = EC_CAPACITY * tokens // N_EXPERTS
    hn, pos, gate = moe_router(x3, wm, sh, router_wt, cap, name + "_router")
    return moe_experts(hn, pos, gate, w1, w3, w2, g2, x3, final_w, cap, nb, final_norm, name + "_experts")


def kernel(x, c, ctx, c_ctx, ada_w, ada_b, norm1_w, norm2_w, ssd_in_w, ssd_conv_w, ssd_conv_b, ssd_dt_bias, ssd_A_log, ssd_D, ssd_norm_w, ssd_out_w, attn_qkv_w, attn_q_norm_w, attn_k_norm_w, attn_out_w, moe_router_w, moe_w1, moe_w3, moe_w2, final_norm_w):
    batch, seq, d = x.shape
    ctx_len = ctx.shape[1]
    depth = ada_w.shape[0]
    assert depth == 2 and d == D_MODEL

    mod_rows = -(-(batch + 1) // SUBLANES) * SUBLANES
    c_all = jnp.zeros((mod_rows, d), F32).at[:batch].set(c).at[batch].set(c_ctx)
    mod = ada_modulation(c_all, ada_w, ada_b)

    def mod_parts(layer):
        parts = jnp.split(mod[layer], 6, axis=-1)
        lat = [p[:batch].reshape(batch, 1, d) for p in parts]
        cx = [p[batch:batch + 1].reshape(1, 1, d) for p in parts]
        return lat, cx

    final_w = final_norm_w.reshape(1, d)
    x2 = x.reshape(batch * seq, d)
    c2 = ctx.reshape(batch * ctx_len, d)

    lat, cx = mod_parts(0)
    n1 = norm1_w[0].reshape(1, 1, d)
    n2 = norm2_w[0].reshape(1, 1, d)
    n_main = SSD_IN_DIM - 2 * SSD_HEADS
    in_w = ssd_in_w[0][:, :n_main].astype(BF16)
    in_w_dt = jnp.pad(ssd_in_w[0][:, n_main:], ((0, 0), (0, LANES - 2 * SSD_HEADS))).astype(BF16)
    proj_l, dtr_l = norm_mod_mm(x2, n1 * (1.0 + lat[1]), lat[0], in_w, seq, 1024, 768, "ssd_in_lat", in_w_dt)
    proj_c, dtr_c = norm_mod_mm(c2, n1 * (1.0 + cx[1]), cx[0], in_w, batch * ctx_len, 1024, 768, "ssd_in_ctx",
                                in_w_dt)
    proj_l3 = proj_l.reshape(batch, seq, n_main)
    proj_c3 = proj_c.reshape(batch, ctx_len, n_main)
    dt_bias = jnp.pad(ssd_dt_bias[0].reshape(1, 2 * SSD_HEADS), ((0, 0), (0, LANES - 2 * SSD_HEADS)))
    a_neg = jnp.pad(-jnp.exp(ssd_A_log[0].astype(F32)).reshape(1, 2 * SSD_HEADS),
                    ((0, 0), (0, LANES - 2 * SSD_HEADS)))
    col_l, row_l = ssd_prep(dtr_l.reshape(batch, seq, LANES), dt_bias, a_neg, "ssd_prep_lat")
    col_c, row_c = ssd_prep(dtr_c.reshape(batch, ctx_len, LANES), dt_bias, a_neg, "ssd_prep_ctx")
    dskip_e = jnp.repeat(ssd_D[0], SSD_HEADDIM).reshape(SSD_GROUPS, 1, SSD_HPG * SSD_HEADDIM)
    y_c, y_l = ssd_scan(proj_c3, proj_l3, ssd_conv_w[0], ssd_conv_b[0].reshape(1, CONV_DIM),
                        col_c, row_c, col_l, row_l, dskip_e)
    ssd_nw = ssd_norm_w[0].reshape(1, D_INNER)
    out_w = ssd_out_w[0].astype(BF16)
    x2 = ssd_out_mm(y_l.reshape(batch * seq, D_INNER), proj_l, ssd_nw, out_w, x2, lat[2], seq, 512,
                    "ssd_out_lat")
    c2 = ssd_out_mm(y_c.reshape(batch * ctx_len, D_INNER), proj_c, ssd_nw, out_w, c2, cx[2],
                    batch * ctx_len, 512, "ssd_out_ctx")

    rwt = moe_router_w[0].T
    w1, w3, w2 = moe_w1[0].astype(BF16), moe_w3[0].astype(BF16), moe_w2[0].astype(BF16)
    x3 = moe_block(x2.reshape(batch, seq, d), n2 * (1.0 + lat[4]), lat[3], lat[5], rwt, w1, w3, w2,
                   final_w, 1, False, "moe0_lat")
    c3 = moe_block(c2.reshape(batch, ctx_len, d), n2 * (1.0 + cx[4]), cx[3], cx[5], rwt, w1, w3, w2,
                   final_w, math.gcd(batch, 8), False, "moe0_ctx")
    x2 = x3.reshape(batch * seq, d)
    c2 = c3.reshape(batch * ctx_len, d)

    lat, cx = mod_parts(1)
    n1 = norm1_w[1].reshape(1, 1, d)
    n2 = norm2_w[1].reshape(1, 1, d)
    qkv_w = attn_qkv_w[0].astype(BF16)
    qkv_l, = norm_mod_mm(x2, n1 * (1.0 + lat[1]), lat[0], qkv_w, seq, 1024, 768, "qkv_lat")
    qkv_c, = norm_mod_mm(c2, n1 * (1.0 + cx[1]), cx[0], qkv_w, batch * ctx_len, 1024, 768, "qkv_ctx")
    tables = _attn_tables(seq)
    q_l, k_l, v_l = qk_prep(qkv_l, attn_q_norm_w[0], attn_k_norm_w[0], tables, seq, True, 512, "qk_prep_lat")
    _, k_c, v_c = qk_prep(qkv_c, attn_q_norm_w[0], attn_k_norm_w[0], tables, ctx_len, False, ctx_len,
                          "qk_prep_ctx")
    o = attention(q_l, k_l, v_l, k_c, v_c, batch, seq, ctx_len, 128)
    x2 = res_mm(o, attn_out_w[0].astype(BF16), x2, lat[2], seq, 1024, "attn_out")

    rwt = moe_router_w[1].T
    w1, w3, w2 = moe_w1[1].astype(BF16), moe_w3[1].astype(BF16), moe_w2[1].astype(BF16)
    x3 = moe_block(x2.reshape(batch, seq, d), n2 * (1.0 + lat[4]), lat[3], lat[5], rwt, w1, w3, w2,
                   final_w, 1, True, "moe1_lat")
    return x3
```

```python
import functools
import math

import numpy as np
import jax
import jax.numpy as jnp
from jax import lax
from jax.experimental import pallas as pl
from jax.experimental.pallas import tpu as pltpu

F32 = jnp.float32
BF16 = jnp.bfloat16
HIGHEST = lax.Precision.HIGHEST

D_MODEL = 1024
GRID_W = 64
EPS = 1e-6
D_INNER = 2048
SSD_HEADDIM = 64
SSD_HEADS = 32
SSD_GROUPS = 8
SSD_HPG = 4
D_STATE = 128
CONV_W = 5
SSD_CHUNK = 128
CONV_DIM = D_INNER + 2 * SSD_GROUPS * D_STATE
SSD_IN_DIM = 2 * D_INNER + 2 * SSD_GROUPS * D_STATE + 2 * SSD_HEADS
HEAD_DIM = 64
N_Q_HEADS = 16
N_KV_HEADS = 4
Q_PER_KV = 4
D_ATTN = 1024
KV_DIM = N_KV_HEADS * HEAD_DIM
QKV_DIM = D_ATTN + 2 * KV_DIM
ROPE_THETA = 10000.0
N_EXPERTS = 16
EC_CAPACITY = 2

LANES = 128
SUBLANES = 8
VMEM_LIMIT_BYTES = 56 * 1024 * 1024


def _cparams(*sem):
    return pltpu.CompilerParams(dimension_semantics=sem, vmem_limit_bytes=VMEM_LIMIT_BYTES)


def _sigmoid(x):
    return 1.0 / (1.0 + jnp.exp(-x))


def _silu(x):
    return x * _sigmoid(x)


def _split2(x):
    hi = x.astype(BF16)
    return hi, (x - hi.astype(F32)).astype(BF16)


def _ada_kernel(c_ref, w_ref, b_ref, o_ref):
    a = _silu(c_ref[...])
    o_ref[...] = jnp.dot(a, w_ref[...], precision=HIGHEST, preferred_element_type=F32) + b_ref[...]


def ada_modulation(c_all, ada_w, ada_b):
    depth, d, n = ada_w.shape
    m = c_all.shape[0]
    tn = 1536
    return pl.pallas_call(
        _ada_kernel,
        grid=(depth, n // tn),
        in_specs=[pl.BlockSpec((m, d), lambda l, j: (0, 0)),
                  pl.BlockSpec((None, d, tn), lambda l, j: (l, 0, j)),
                  pl.BlockSpec((None, 1, tn), lambda l, j: (l, 0, j))],
        out_specs=pl.BlockSpec((None, m, tn), lambda l, j: (l, 0, j)),
        out_shape=jax.ShapeDtypeStruct((depth, m, n), F32),
        compiler_params=_cparams("parallel", "parallel"),
        name="ada_modulation",
    )(c_all, ada_w, ada_b.reshape(depth, 1, n))


def _norm_mod(x_ref, wm_ref, sh_ref):
    x = x_ref[...]
    ms = jnp.mean(x * x, axis=-1, keepdims=True)
    return (x * lax.rsqrt(ms + EPS) * wm_ref[...] + sh_ref[...]).astype(BF16)


def _norm_mod_mm_kernel(x_ref, wm_ref, sh_ref, w_ref, o_ref, a_scr):
    @pl.when(pl.program_id(1) == 0)
    def _():
        a_scr[...] = _norm_mod(x_ref, wm_ref, sh_ref)

    o_ref[...] = jnp.dot(a_scr[...], w_ref[...], preferred_element_type=F32).astype(BF16)


def _norm_mod_mm2_kernel(x_ref, wm_ref, sh_ref, w_ref, wf_ref, o_ref, of_ref, a_scr):
    @pl.when(pl.program_id(1) == 0)
    def _():
        a_scr[...] = _norm_mod(x_ref, wm_ref, sh_ref)
        of_ref[...] = jnp.dot(a_scr[...], wf_ref[...], preferred_element_type=F32)

    o_ref[...] = jnp.dot(a_scr[...], w_ref[...], preferred_element_type=F32).astype(BF16)


def norm_mod_mm(x2d, wm, sh, w, rows_per_group, tm, tn, name, w_f32out=None):
    m, k = x2d.shape
    n = w.shape[1]
    tiles_per_group = rows_per_group // tm
    in_specs = [pl.BlockSpec((tm, k), lambda i, j: (i, 0)),
                pl.BlockSpec((None, 1, k), lambda i, j: (i // tiles_per_group, 0, 0)),
                pl.BlockSpec((None, 1, k), lambda i, j: (i // tiles_per_group, 0, 0)),
                pl.BlockSpec((k, tn), lambda i, j: (0, j))]
    out_specs = [pl.BlockSpec((tm, tn), lambda i, j: (i, j))]
    out_shape = [jax.ShapeDtypeStruct((m, n), BF16)]
    args = [x2d, wm, sh, w]
    body = _norm_mod_mm_kernel
    if w_f32out is not None:
        nf = w_f32out.shape[1]
        in_specs.append(pl.BlockSpec((k, nf), lambda i, j: (0, 0)))
        out_specs.append(pl.BlockSpec((tm, nf), lambda i, j: (i, 0)))
        out_shape.append(jax.ShapeDtypeStruct((m, nf), F32))
        args.append(w_f32out)
        body = _norm_mod_mm2_kernel
    return pl.pallas_call(
        body,
        grid=(m // tm, n // tn),
        in_specs=in_specs, out_specs=out_specs, out_shape=out_shape,
        scratch_shapes=[pltpu.VMEM((tm, k), BF16)],
        compiler_params=_cparams("parallel", "arbitrary"),
        name=name,
    )(*args)


def _res_mm_kernel(a_ref, w_ref, res_ref, gate_ref, o_ref):
    acc = jnp.dot(a_ref[...], w_ref[...], preferred_element_type=F32)
    o_ref[...] = res_ref[...] + gate_ref[...] * acc


def res_mm(a, w, res, gate, rows_per_group, tm, name):
    m, k = a.shape
    n = w.shape[1]
    tiles_per_group = rows_per_group // tm
    return pl.pallas_call(
        _res_mm_kernel,
        grid=(m // tm,),
        in_specs=[pl.BlockSpec((tm, k), lambda i: (i, 0)),
                  pl.BlockSpec((k, n), lambda i: (0, 0)),
                  pl.BlockSpec((tm, n), lambda i: (i, 0)),
                  pl.BlockSpec((None, 1, n), lambda i: (i // tiles_per_group, 0, 0))],
        out_specs=pl.BlockSpec((tm, n), lambda i: (i, 0)),
        out_shape=jax.ShapeDtypeStruct((m, n), F32),
        compiler_params=_cparams("parallel"),
        name=name,
    )(a, w, res, gate)


SSD_OUT_SUB_ROWS = 512


def _ssd_out_kernel(y_ref, z_ref, w_ref, res_ref, gate_ref, o_ref):
    for r in range(y_ref.shape[0] // SSD_OUT_SUB_ROWS):
        rows = slice(r * SSD_OUT_SUB_ROWS, (r + 1) * SSD_OUT_SUB_ROWS)
        v = y_ref[rows, :].astype(F32) * _silu(z_ref[rows, :].astype(F32))
        ms = jnp.mean(v * v, axis=-1, keepdims=True)
        acc = jnp.dot(v.astype(BF16), w_ref[...], preferred_element_type=F32)
        o_ref[rows, :] = res_ref[rows, :] + gate_ref[...] * (lax.rsqrt(ms + EPS) * acc)


def ssd_out_mm(y, proj, w_normed, res, gate, rows_per_group, tm, name):
    m, k = y.shape
    n = w_normed.shape[1]
    tiles_per_group = rows_per_group // tm
    return pl.pallas_call(
        _ssd_out_kernel,
        grid=(m // tm,),
        in_specs=[pl.BlockSpec((tm, k), lambda i: (i, 0)),
                  pl.BlockSpec((tm, k), lambda i: (i, 0)),
                  pl.BlockSpec((k, n), lambda i: (0, 0)),
                  pl.BlockSpec((tm, n), lambda i: (i, 0)),
                  pl.BlockSpec((None, 1, n), lambda i: (i // tiles_per_group, 0, 0))],
        out_specs=pl.BlockSpec((tm, n), lambda i: (i, 0)),
        out_shape=jax.ShapeDtypeStruct((m, n), F32),
        compiler_params=_cparams("parallel"),
        name=name,
    )(y, proj, w_normed, res, gate)


def _ssd_prep_kernel(dtr_ref, bias_ref, a_ref, scol_ref, srow_ref, col_ref, row_ref):
    seq = dtr_ref.shape[0]
    x = dtr_ref[...] + bias_ref[...]
    dt = jnp.maximum(x, 0.0) + jnp.log1p(jnp.exp(-jnp.abs(x)))
    a = dt * a_ref[...]
    t_i = lax.broadcasted_iota(jnp.int32, (SSD_CHUNK, SSD_CHUNK), 0)
    s_i = lax.broadcasted_iota(jnp.int32, (SSD_CHUNK, SSD_CHUNK), 1)
    tri_lo = (s_i <= t_i).astype(F32)
    tri_up = (s_i >= t_i).astype(F32)
    fwd_lane = lax.broadcasted_iota(jnp.int32, (SSD_CHUNK, LANES), 1) < SSD_HEADS
    chunks = []
    for c in range(seq // SSD_CHUNK):
        ac = a[c * SSD_CHUNK:(c + 1) * SSD_CHUNK]
        cf = jnp.dot(tri_lo, ac, precision=HIGHEST, preferred_element_type=F32)
        cb = jnp.dot(tri_up, ac, precision=HIGHEST, preferred_element_type=F32)
        chunks.append(jnp.where(fwd_lane, cf, cb))
    cum = jnp.concatenate(chunks, axis=0)
    both = jnp.concatenate([dt, cum], axis=1)
    hi = both.astype(BF16)
    rest = both - hi.astype(F32)
    mid = rest.astype(BF16)
    lo = (rest - mid.astype(F32)).astype(BF16)
    scol = scol_ref[...]
    col_ref[...] = (jnp.dot(hi, scol, preferred_element_type=F32)
                    + jnp.dot(mid, scol, preferred_element_type=F32)
                    + jnp.dot(lo, scol, preferred_element_type=F32))
    row_ref[...] = lax.dot_general(srow_ref[...], cum, (((1,), (1,)), ((), ())),
                                   precision=HIGHEST, preferred_element_type=F32)


def _ssd_select_tables():
    scol = np.zeros((2 * LANES, SSD_GROUPS * LANES), np.float32)
    srow = np.zeros((SSD_GROUPS * SUBLANES, LANES), np.float32)
    for g in range(SSD_GROUPS):
        for d in range(2):
            for r in range(SSD_HPG):
                lane = d * SSD_HEADS + SSD_HPG * g + r
                scol[lane, g * LANES + d * SSD_HPG + r] = 1.0
                scol[LANES + lane, g * LANES + 2 * SSD_HPG + d * SSD_HPG + r] = 1.0
                srow[g * SUBLANES + d * SSD_HPG + r, lane] = 1.0
    return jnp.asarray(scol, dtype=BF16), jnp.asarray(srow)


def ssd_prep(dt_raw3, dt_bias, a_neg, name):
    b, seq, _ = dt_raw3.shape
    scol, srow = _ssd_select_tables()
    return pl.pallas_call(
        _ssd_prep_kernel,
        grid=(b,),
        in_specs=[pl.BlockSpec((None, seq, LANES), lambda i: (i, 0, 0)),
                  pl.BlockSpec((1, LANES), lambda i: (0, 0)),
                  pl.BlockSpec((1, LANES), lambda i: (0, 0)),
                  pl.BlockSpec(scol.shape, lambda i: (0, 0)),
                  pl.BlockSpec(srow.shape, lambda i: (0, 0))],
        out_specs=[pl.BlockSpec((None, seq, SSD_GROUPS * LANES), lambda i: (i, 0, 0)),
                   pl.BlockSpec((None, SSD_GROUPS * SUBLANES, seq), lambda i: (i, 0, 0))],
        out_shape=[jax.ShapeDtypeStruct((b, seq, SSD_GROUPS * LANES), F32),
                   jax.ShapeDtypeStruct((b, SSD_GROUPS * SUBLANES, seq), F32)],
        compiler_params=_cparams("parallel"),
        name=name,
    )(dt_raw3, dt_bias, a_neg, scol, srow)


CONV_HALO = 2 * SUBLANES
CONV_TAPS_OFF_CENTRE = (0, 1, 3, 4)
SSD_UNROLL = 4


def _ssd_scan_kernel(xc_ref, bc_ref, cc_ref, xl_ref, bl_ref, cl_ref,
                     cwx_ref, cwb_ref, cwc_ref, cbx_ref, cbb_ref, cbc_ref,
                     colc_ref, rowc_ref, coll_ref, rowl_ref, dsk_ref,
                     yc_ref, yl_ref,
                     pad_scr, xs_scr, bt_scr, cs_scr, y_scr, xdt_scr, xdw_scr, ecum_scr, seg_scr, ed_scr,
                     hf_scr, hb_scr):
    q = SSD_CHUNK
    width = SSD_HPG * SSD_HEADDIM
    win_rows = q + 2 * CONV_HALO
    dsk = dsk_ref[...]

    r_i = lax.broadcasted_iota(jnp.int32, (len(CONV_TAPS_OFF_CENTRE) * q, win_rows), 0)
    j_i = lax.broadcasted_iota(jnp.int32, (len(CONV_TAPS_OFF_CENTRE) * q, win_rows), 1)
    tap_of_row = jnp.zeros_like(r_i)
    for n, k in enumerate(CONV_TAPS_OFF_CENTRE):
        tap_of_row = jnp.where(r_i // q == n, k, tap_of_row)
    shift_mat = jnp.where(j_i == r_i % q + CONV_HALO - CONV_W // 2 + tap_of_row, 1.0, 0.0).astype(BF16)

    def conv_silu(parts, w, bias, seq, store):
        zeros = jnp.zeros((CONV_HALO, width), BF16)
        pad_scr[0:CONV_HALO, :] = zeros
        pad_scr[CONV_HALO + seq:2 * CONV_HALO + seq, :] = zeros

        def copy_body(c, carry):
            base = pl.multiple_of(c * q, q)
            dst = pl.multiple_of(base + CONV_HALO, CONV_HALO)
            for raw_ref, lane0 in parts:
                pad_scr[pl.ds(dst, q), lane0:lane0 + raw_ref.shape[1]] = raw_ref[pl.ds(base, q), :]
            return carry

        lax.fori_loop(0, seq // q, copy_body, 0)

        def conv_body(c, carry):
            base = pl.multiple_of(c * q, q)
            win = pad_scr[pl.ds(base, win_rows), :]
            shifted = jnp.dot(shift_mat, win, preferred_element_type=F32)
            acc = bias + win[CONV_HALO:CONV_HALO + q].astype(F32) * w[CONV_W // 2:CONV_W // 2 + 1]
            for n, k in enumerate(CONV_TAPS_OFF_CENTRE):
                acc = acc + shifted[n * q:(n + 1) * q] * w[k:k + 1]
            store(base, _silu(acc))
            return carry

        lax.fori_loop(0, seq // q, conv_body, 0, unroll=min(SSD_UNROLL, seq // q))

    def x_store(base, v):
        xs_scr[pl.ds(base, q), :] = v
        y_scr[pl.ds(base, q), :] = dsk * v

    def bc_store(base, v):
        bt_scr[:, pl.ds(base, q)] = v[:, 0:D_STATE].T.astype(BF16)
        cs_scr[pl.ds(base, q), :] = v[:, D_STATE:2 * D_STATE].astype(BF16)

    w_bc = jnp.concatenate([cwb_ref[...], cwc_ref[...]], axis=1)
    b_bc = jnp.concatenate([cbb_ref[...], cbc_ref[...]], axis=1)

    t_i = lax.broadcasted_iota(jnp.int32, (q, q), 0)
    s_i = lax.broadcasted_iota(jnp.int32, (q, q), 1)
    head_of_lane = lax.broadcasted_iota(jnp.int32, (q, width), 1) // SSD_HEADDIM

    def per_head(parts):
        out = parts[SSD_HPG - 1]
        for r in range(SSD_HPG - 2, -1, -1):
            out = jnp.where(head_of_lane == r, parts[r], out)
        return out

    src_lane = lax.broadcasted_iota(jnp.int32, (LANES, 2 * width), 0)
    dst_head = lax.broadcasted_iota(jnp.int32, (LANES, 2 * width), 1) // SSD_HEADDIM
    dt_spread = jnp.where(src_lane == dst_head, 1.0, 0.0).astype(BF16)

    first_head_lanes = lax.broadcasted_iota(jnp.int32, (q, q), 1) < SSD_HEADDIM

    def per_head_lanes(parts):
        return jnp.concatenate([jnp.where(first_head_lanes, parts[0], parts[1]),
                                jnp.where(first_head_lanes, parts[2], parts[3])], axis=1)

    def run_seq(x_ref, b_ref, c_ref, col_ref, row_ref, y_ref):
        seq = x_ref.shape[0]
        nchunks = seq // q
        conv_silu([(x_ref, 0)], cwx_ref[...], cbx_ref[...], seq, x_store)
        conv_silu([(b_ref, 0), (c_ref, D_STATE)], w_bc, b_bc, seq, bc_store)

        def decay_terms(c, d, cp, dt_e):
            mask = (s_i <= t_i) if d == 0 else (s_i >= t_i)
            edge = q - 1 if d == 0 else 0
            base = pl.multiple_of(c * q, q)
            rp = row_ref[:, pl.ds(base, q)]
            cum_b = [jnp.broadcast_to(cp[:, 2 * SSD_HPG + SSD_HPG * d + r:2 * SSD_HPG + SSD_HPG * d + r + 1],
                                      (q, q)) for r in range(SSD_HPG)]
            cum_e = per_head_lanes(cum_b)
            xdt = xs_scr[pl.ds(base, q), :] * dt_e
            cum_edge = cum_e[edge:edge + 1, :]
            xdt_scr[d, pl.ds(base, q), :] = xdt.astype(BF16)
            xdw_scr[d, pl.ds(base, q), :] = (xdt * jnp.exp(cum_edge - cum_e)).astype(BF16)
            ecum_scr[d, pl.ds(base, q), :] = jnp.exp(cum_e)
            ed_scr[d, pl.ds(pl.multiple_of(c * SUBLANES, SUBLANES), SUBLANES), :] = jnp.broadcast_to(
                jnp.exp(cum_edge), (SUBLANES, width))
            for r in range(SSD_HPG):
                row = rp[SSD_HPG * d + r:SSD_HPG * d + r + 1, :]
                seg = jnp.exp(jnp.where(mask, cum_b[r] - row, -1e30))
                seg_scr[SSD_HPG * d + r, pl.ds(base, q), :] = seg.astype(BF16)

        def prep_body(c, carry):
            cp = col_ref[pl.ds(pl.multiple_of(c * q, q), q), :]
            cp_hi, cp_lo = _split2(cp)
            dt_all = (jnp.dot(cp_hi, dt_spread, preferred_element_type=F32)
                      + jnp.dot(cp_lo, dt_spread, preferred_element_type=F32))
            decay_terms(c, 0, cp, dt_all[:, 0:width])
            decay_terms(c, 1, cp, dt_all[:, width:2 * width])
            return carry

        lax.fori_loop(0, nchunks, prep_body, 0, unroll=min(SSD_UNROLL, nchunks))

        def chunk(c, d, h_scr):
            base = pl.multiple_of(c * q, q)
            bq_t = bt_scr[:, pl.ds(base, q)]
            cq = cs_scr[pl.ds(base, q), :]
            cb = jnp.dot(cq, bq_t, preferred_element_type=F32).astype(BF16)
            ms = [cb * seg_scr[SSD_HPG * d + r, pl.ds(base, q), :] for r in range(SSD_HPG)]
            stacked = jnp.dot(jnp.concatenate(ms, axis=0), xdt_scr[d, pl.ds(base, q), :],
                              preferred_element_type=F32)
            y = per_head([stacked[r * q:(r + 1) * q] for r in range(SSD_HPG)])
            h_t = h_scr[...]
            y = y + jnp.dot(cq, h_t.astype(BF16), preferred_element_type=F32) * ecum_scr[d, pl.ds(base, q), :]
            edge_decay = ed_scr[d, pl.ds(pl.multiple_of(c * SUBLANES, SUBLANES), SUBLANES), :][0:1]
            h_scr[...] = h_t * edge_decay + jnp.dot(bq_t, xdw_scr[d, pl.ds(base, q), :],
                                                    preferred_element_type=F32)
            y_scr[pl.ds(base, q), :] += y

        def scan_body(i, carry):
            chunk(i, 0, hf_scr)
            chunk(nchunks - 1 - i, 1, hb_scr)
            return carry

        lax.fori_loop(0, nchunks, scan_body, 0, unroll=min(SSD_UNROLL, nchunks))

        def out_body(c, carry):
            base = pl.multiple_of(c * q, q)
            y_ref[pl.ds(base, q), :] = y_scr[pl.ds(base, q), :].astype(BF16)
            return carry

        lax.fori_loop(0, nchunks, out_body, 0)

    hf_scr[...] = jnp.zeros(hf_scr.shape, F32)
    hb_scr[...] = jnp.zeros(hb_scr.shape, F32)
    run_seq(xc_ref, bc_ref, cc_ref, colc_ref, rowc_ref, yc_ref)
    run_seq(xl_ref, bl_ref, cl_ref, coll_ref, rowl_ref, yl_ref)


def ssd_scan(proj_c, proj_l, conv_w, conv_b, col_c, row_c, col_l, row_l, dskip_e):
    b, lc, _ = proj_c.shape
    ll = proj_l.shape[1]
    width = SSD_HPG * SSD_HEADDIM
    x0 = D_INNER // width
    b0 = (2 * D_INNER) // D_STATE
    c0 = b0 + SSD_GROUPS
    wx0, wb0, wc0 = 0, D_INNER // D_STATE, D_INNER // D_STATE + SSD_GROUPS

    def seq_specs(seq):
        return [pl.BlockSpec((None, seq, width), lambda i, g: (i, 0, x0 + g)),
                pl.BlockSpec((None, seq, D_STATE), lambda i, g: (i, 0, b0 + g)),
                pl.BlockSpec((None, seq, D_STATE), lambda i, g: (i, 0, c0 + g))]

    def par_specs(rows):
        return [pl.BlockSpec((rows, width), lambda i, g: (0, wx0 + g)),
                pl.BlockSpec((rows, D_STATE), lambda i, g: (0, wb0 + g)),
                pl.BlockSpec((rows, D_STATE), lambda i, g: (0, wc0 + g))]

    def pack_specs(seq):
        return [pl.BlockSpec((None, seq, LANES), lambda i, g: (i, 0, g)),
                pl.BlockSpec((None, SUBLANES, seq), lambda i, g: (i, g, 0))]

    return pl.pallas_call(
        _ssd_scan_kernel,
        grid=(b, SSD_GROUPS),
        in_specs=(seq_specs(lc) + seq_specs(ll) + par_specs(CONV_W) + par_specs(1)
                  + pack_specs(lc) + pack_specs(ll)
                  + [pl.BlockSpec((None, 1, width), lambda i, g: (g, 0, 0))]),
        out_specs=[pl.BlockSpec((None, lc, width), lambda i, g: (i, 0, g)),
                   pl.BlockSpec((None, ll, width), lambda i, g: (i, 0, g))],
        out_shape=[jax.ShapeDtypeStruct((b, lc, D_INNER), BF16),
                   jax.ShapeDtypeStruct((b, ll, D_INNER), BF16)],
        scratch_shapes=[pltpu.VMEM((ll + 2 * CONV_HALO, width), BF16),
                        pltpu.VMEM((ll, width), F32),
                        pltpu.VMEM((D_STATE, ll), BF16),
                        pltpu.VMEM((ll, D_STATE), BF16),
                        pltpu.VMEM((ll, width), F32),
                        pltpu.VMEM((2, ll, width), BF16),
                        pltpu.VMEM((2, ll, width), BF16),
                        pltpu.VMEM((2, ll, width), F32),
                        pltpu.VMEM((2 * SSD_HPG, ll, SSD_CHUNK), BF16),
                        pltpu.VMEM((2, ll // SSD_CHUNK * SUBLANES, width), F32),
                        pltpu.VMEM((D_STATE, width), F32), pltpu.VMEM((D_STATE, width), F32)],
        compiler_params=_cparams("parallel", "parallel"),
        name="ssd_scan",
    )(proj_c, proj_c, proj_c, proj_l, proj_l, proj_l,
      conv_w, conv_w, conv_w, conv_b, conv_b, conv_b,
      col_c, row_c, col_l, row_l, dskip_e)


def _qk_prep_kernel(qkv_ref, gmat_ref, qw_ref, kw_ref, rope_c_ref, rope_a_ref, rope_b_ref, rep_ref,
                    q_ref, k_ref, v_ref, *, use_rope):
    blk = Q_PER_KV * HEAD_DIM
    gmat = gmat_ref[...]

    def norm_rope(x, w):
        sq_hi, sq_lo = _split2(x * x)
        ms = (jnp.dot(sq_hi, gmat, preferred_element_type=F32)
              + jnp.dot(sq_lo, gmat, preferred_element_type=F32))
        xn = x * lax.rsqrt(ms + EPS) * w
        if use_rope:
            half = HEAD_DIM // 2
            xn = (xn * rope_c_ref[...] + pltpu.roll(xn, blk - half, 1) * rope_a_ref[...]
                  + pltpu.roll(xn, half, 1) * rope_b_ref[...])
        return xn

    scale = math.log2(math.e) / math.sqrt(HEAD_DIM)
    for j in range(D_ATTN // blk):
        xq = norm_rope(qkv_ref[:, j * blk:(j + 1) * blk].astype(F32), qw_ref[...])
        q_ref[:, j * blk:(j + 1) * blk] = (xq * scale).astype(BF16)
    xk = norm_rope(qkv_ref[:, D_ATTN:D_ATTN + KV_DIM].astype(F32), kw_ref[...]).astype(BF16)
    xv = qkv_ref[:, D_ATTN + KV_DIM:D_ATTN + 2 * KV_DIM]
    for kv in range(N_KV_HEADS):
        rep = rep_ref[kv]
        k_ref[:, kv * blk:(kv + 1) * blk] = jnp.dot(xk, rep, preferred_element_type=F32).astype(BF16)
        v_ref[:, kv * blk:(kv + 1) * blk] = jnp.dot(xv, rep, preferred_element_type=F32).astype(BF16)


def _attn_tables(seq):
    rows = seq // GRID_W
    row = np.repeat(np.arange(rows), GRID_W).astype(np.float32)
    col = np.tile(np.arange(GRID_W), rows).astype(np.float32)
    axis_dim = HEAD_DIM // 2
    inv_freq = jnp.asarray(ROPE_THETA, F32) ** (-jnp.arange(0, axis_dim, 2, dtype=F32) / axis_dim)
    ang = jnp.concatenate([jnp.asarray(row)[:, None] * inv_freq, jnp.asarray(col)[:, None] * inv_freq], axis=-1)
    cos, sin = jnp.cos(ang), jnp.sin(ang)
    zero = jnp.zeros_like(sin)
    reps = Q_PER_KV
    rope_c = jnp.tile(jnp.concatenate([cos, cos], axis=-1), (1, reps))
    rope_a = jnp.tile(jnp.concatenate([-sin, zero], axis=-1), (1, reps))
    rope_b = jnp.tile(jnp.concatenate([zero, sin], axis=-1), (1, reps))
    return rope_c, rope_a, rope_b


def _attn_consts():
    blk = Q_PER_KV * HEAD_DIM
    head = np.arange(blk) // HEAD_DIM
    gmat = (head[:, None] == head[None, :]).astype(np.float32) / HEAD_DIM
    rep = np.zeros((N_KV_HEADS, KV_DIM, blk), np.float32)
    for kv in range(N_KV_HEADS):
        for j in range(blk):
            rep[kv, kv * HEAD_DIM + j % HEAD_DIM, j] = 1.0
    return jnp.asarray(gmat, dtype=BF16), jnp.asarray(rep, dtype=BF16)


def qk_prep(qkv, q_norm_w, k_norm_w, tables, seq, use_rope, tl, name):
    m = qkv.shape[0]
    blk = Q_PER_KV * HEAD_DIM
    gmat, rep = _attn_consts()
    qw = jnp.tile(q_norm_w.reshape(1, HEAD_DIM), (1, Q_PER_KV))
    kw = jnp.tile(k_norm_w.reshape(1, HEAD_DIM), (1, N_KV_HEADS))
    tiles_per_seq = seq // tl
    tab_spec = pl.BlockSpec((tl, blk), lambda i: (i % tiles_per_seq, 0))
    out_w = N_KV_HEADS * blk
    return pl.pallas_call(
        functools.partial(_qk_prep_kernel, use_rope=use_rope),
        grid=(m // tl,),
        in_specs=[pl.BlockSpec((tl, QKV_DIM), lambda i: (i, 0)),
                  pl.BlockSpec((blk, blk), lambda i: (0, 0)),
                  pl.BlockSpec((1, blk), lambda i: (0, 0)),
                  pl.BlockSpec((1, blk), lambda i: (0, 0)),
                  tab_spec, tab_spec, tab_spec,
                  pl.BlockSpec(rep.shape, lambda i: (0, 0, 0))],
        out_specs=[pl.BlockSpec((tl, D_ATTN), lambda i: (i, 0)),
                   pl.BlockSpec((tl, out_w), lambda i: (i, 0)),
                   pl.BlockSpec((tl, out_w), lambda i: (i, 0))],
        out_shape=[jax.ShapeDtypeStruct((m, D_ATTN), BF16),
                   jax.ShapeDtypeStruct((m, out_w), BF16),
                   jax.ShapeDtypeStruct((m, out_w), BF16)],
        compiler_params=_cparams("parallel"),
        name=name,
    )(qkv, gmat, qw, kw, *tables, rep)


def _attn_kernel(q_ref, kl_ref, vl_ref, kc_ref, vc_ref, o_ref):
    tq = q_ref.shape[0]
    blk = Q_PER_KV * HEAD_DIM
    head_of_lane = lax.broadcasted_iota(jnp.int32, (tq, blk), 1) // HEAD_DIM
    nt = (((1,), (1,)), ((), ()))
    for kv in range(N_KV_HEADS):
        cols = slice(kv * blk, (kv + 1) * blk)
        q4 = q_ref[:, cols]
        zero = jnp.zeros_like(q4)
        qs = jnp.concatenate([jnp.where(head_of_lane == g, q4, zero) for g in range(Q_PER_KV)], axis=0)
        half = kl_ref.shape[0] // 2
        s_a = lax.dot_general(qs, kl_ref[0:half, cols], nt, preferred_element_type=F32)
        s_b = lax.dot_general(qs, kl_ref[half:2 * half, cols], nt, preferred_element_type=F32)
        s_c = lax.dot_general(qs, kc_ref[:, cols], nt, preferred_element_type=F32)
        m = jnp.maximum(jnp.maximum(jnp.max(s_a, axis=-1, keepdims=True), jnp.max(s_b, axis=-1, keepdims=True)),
                        jnp.max(s_c, axis=-1, keepdims=True))
        p_a = jnp.exp2(s_a - m)
        p_b = jnp.exp2(s_b - m)
        p_c = jnp.exp2(s_c - m)
        denom = (jnp.sum(p_a, axis=-1, keepdims=True) + jnp.sum(p_b, axis=-1, keepdims=True)
                 + jnp.sum(p_c, axis=-1, keepdims=True))
        rows = Q_PER_KV * tq // 2
        o_parts = []
        for h in range(2):
            rs = slice(h * rows, (h + 1) * rows)
            o_parts.append(
                jnp.dot(p_a[rs].astype(BF16), vl_ref[0:half, cols], preferred_element_type=F32)
                + jnp.dot(p_b[rs].astype(BF16), vl_ref[half:2 * half, cols], preferred_element_type=F32)
                + jnp.dot(p_c[rs].astype(BF16), vc_ref[:, cols], preferred_element_type=F32))
        o = jnp.concatenate(o_parts, axis=0) / denom
        out = o[(Q_PER_KV - 1) * tq:Q_PER_KV * tq]
        for g in range(Q_PER_KV - 2, -1, -1):
            out = jnp.where(head_of_lane == g, o[g * tq:(g + 1) * tq], out)
        o_ref[:, cols] = out.astype(BF16)


def attention(q, k_lat, v_lat, k_ctx, v_ctx, batch, seq, ctx_len, tq):
    width = q.shape[1]
    nq = seq // tq
    return pl.pallas_call(
        _attn_kernel,
        grid=(batch, nq),
        in_specs=[pl.BlockSpec((tq, width), lambda b, i: (b * nq + i, 0)),
                  pl.BlockSpec((seq, width), lambda b, i: (b, 0)),
                  pl.BlockSpec((seq, width), lambda b, i: (b, 0)),
                  pl.BlockSpec((ctx_len, width), lambda b, i: (b, 0)),
                  pl.BlockSpec((ctx_len, width), lambda b, i: (b, 0))],
        out_specs=pl.BlockSpec((tq, width), lambda b, i: (b * nq + i, 0)),
        out_shape=jax.ShapeDtypeStruct((batch * seq, width), BF16),
        compiler_params=_cparams("parallel", "parallel"),
        name="attention",
    )(q, k_lat, v_lat, k_ctx, v_ctx)


def _router_kernel(x_ref, wm_ref, sh_ref, rw_ref, hn_ref, pos_ref, gate_ref, *, cap):
    tokens = x_ref.shape[0]
    x = x_ref[...]
    ms = jnp.mean(x * x, axis=-1, keepdims=True)
    hn = x * lax.rsqrt(ms + EPS) * wm_ref[...] + sh_ref[...]
    hn_hi, hn_lo = _split2(hn)
    hn_ref[...] = hn_hi
    rw_hi, rw_lo = _split2(rw_ref[...])
    nt = (((1,), (1,)), ((), ()))
    logits = (lax.dot_general(rw_hi, hn_hi, nt, preferred_element_type=F32)
              + lax.dot_general(rw_hi, hn_lo, nt, preferred_element_type=F32)
              + lax.dot_general(rw_lo, hn_hi, nt, preferred_element_type=F32))
    e = jnp.exp(logits - jnp.max(logits, axis=0, keepdims=True))
    aff = e / jnp.sum(e, axis=0, keepdims=True)
    gate_ref[...] = aff
    bits = pltpu.bitcast(aff, jnp.int32)

    def count(flags):
        return jnp.sum(jnp.where(flags, 1.0, 0.0), axis=1, keepdims=True)

    def search(i, thr):
        cand = thr | jnp.left_shift(jnp.int32(1), 30 - i)
        return jnp.where(count(bits >= cand) >= cap, cand, thr)

    thr = lax.fori_loop(0, 31, search, jnp.zeros((N_EXPERTS, 1), jnp.int32))
    above = bits > thr
    tied = bits == thr
    need = cap - count(above)

    t_i = lax.broadcasted_iota(jnp.int32, (LANES, LANES), 0)
    s_i = lax.broadcasted_iota(jnp.int32, (LANES, LANES), 1)
    tri = (t_i <= s_i).astype(BF16)

    def lane_cumsum(flags):
        carry = jnp.zeros((N_EXPERTS, 1), F32)
        parts = []
        for c in range(tokens // LANES):
            part = jnp.dot(flags[:, c * LANES:(c + 1) * LANES].astype(BF16), tri,
                           preferred_element_type=F32) + carry
            parts.append(part)
            carry = part[:, LANES - 1:LANES]
        return jnp.concatenate(parts, axis=1)

    tie_rank = lane_cumsum(jnp.where(tied, 1.0, 0.0))
    sel = above | (tied & (tie_rank <= need))
    slot = lane_cumsum(jnp.where(sel, 1.0, 0.0)) - 1.0
    pos_ref[...] = jnp.where(sel, slot, -1.0).astype(jnp.int32)


def moe_router(x3, wm, sh, router_wt, cap, name):
    b, tokens, d = x3.shape
    per_batch = wm.shape[0] > 1
    mod_spec = pl.BlockSpec((None, 1, d), (lambda i: (i, 0, 0)) if per_batch else (lambda i: (0, 0, 0)))
    return pl.pallas_call(
        functools.partial(_router_kernel, cap=cap),
        grid=(b,),
        in_specs=[pl.BlockSpec((None, tokens, d), lambda i: (i, 0, 0)),
                  mod_spec, mod_spec,
                  pl.BlockSpec((N_EXPERTS, d), lambda i: (0, 0))],
        out_specs=[pl.BlockSpec((None, tokens, d), lambda i: (i, 0, 0)),
                   pl.BlockSpec((None, N_EXPERTS, tokens), lambda i: (i, 0, 0)),
                   pl.BlockSpec((None, N_EXPERTS, tokens), lambda i: (i, 0, 0))],
        out_shape=[jax.ShapeDtypeStruct((b, tokens, d), BF16),
                   jax.ShapeDtypeStruct((b, N_EXPERTS, tokens), jnp.int32),
                   jax.ShapeDtypeStruct((b, N_EXPERTS, tokens), F32)],
        compiler_params=_cparams("parallel"),
        name=name,
    )(x3, wm, sh, router_wt)


def _expert_kernel(hn_ref, pos_ref, gate_ref, w1_ref, w3_ref, w2_ref, g2_ref, x_ref, fw_ref, o_ref,
                   *, cap, final_norm):
    nb, tokens, _ = hn_ref.shape
    e = pl.program_id(1)

    @pl.when(e == 0)
    def _():
        o_ref[...] = x_ref[...]

    slot_i = lax.broadcasted_iota(jnp.int32, (cap, tokens), 0)
    onehots, gathered, gates = [], [], []
    for n in range(nb):
        hit = slot_i == pos_ref[n]
        onehot = jnp.where(hit, 1.0, 0.0).astype(BF16)
        onehots.append(onehot)
        gates.append(jnp.sum(jnp.where(hit, gate_ref[n], 0.0), axis=1, keepdims=True))
        gathered.append(jnp.dot(onehot, hn_ref[n], preferred_element_type=F32).astype(BF16))
    xg = jnp.concatenate(gathered, axis=0) if nb > 1 else gathered[0]
    a = jnp.dot(xg, w1_ref[...], preferred_element_type=F32)
    g = jnp.dot(xg, w3_ref[...], preferred_element_type=F32)
    y = jnp.dot((_silu(a) * g).astype(BF16), w2_ref[...], preferred_element_type=F32)
    for n in range(nb):
        yn = (y[n * cap:(n + 1) * cap] * gates[n] * g2_ref[n]).astype(BF16)
        o_ref[n] += lax.dot_general(onehots[n], yn, (((0,), (0,)), ((), ())),
                                    preferred_element_type=F32)

    if final_norm:
        @pl.when(e == pl.num_programs(1) - 1)
        def _():
            for n in range(nb):
                v = o_ref[n]
                ms = jnp.mean(v * v, axis=-1, keepdims=True)
                o_ref[n] = v * lax.rsqrt(ms + EPS) * fw_ref[...]


def moe_experts(hn, pos, gate, layer, w1, w3, w2, g2, x3, final_w, cap, nb, final_norm, name):
    b, tokens, d = hn.shape
    g2 = jnp.broadcast_to(g2, (b, 1, d))
    g2_spec = pl.BlockSpec((nb, 1, d), lambda i, e: (i, 0, 0))
    pos4 = pos.reshape(b, N_EXPERTS, 1, tokens)
    gate4 = gate.reshape(b, N_EXPERTS, 1, tokens)
    sel_spec = pl.BlockSpec((nb, None, 1, tokens), lambda i, e: (i, e, 0, 0))
    w_spec = pl.BlockSpec((None, None, d, d), lambda i, e: (layer, e, 0, 0))
    tok_spec = pl.BlockSpec((nb, tokens, d), lambda i, e: (i, 0, 0))
    acc_spec = pl.BlockSpec((nb, tokens, d), lambda i, e: (i, 0, 0), pipeline_mode=pl.Buffered(1))
    return pl.pallas_call(
        functools.partial(_expert_kernel, cap=cap, final_norm=final_norm),
        grid=(b // nb, N_EXPERTS),
        in_specs=[tok_spec, sel_spec, sel_spec, w_spec, w_spec, w_spec, g2_spec, tok_spec,
                  pl.BlockSpec((1, d), lambda i, e: (0, 0))],
        out_specs=acc_spec,
        out_shape=jax.ShapeDtypeStruct((b, tokens, d), F32),
        compiler_params=_cparams("parallel", "arbitrary"),
        name=name,
    )(hn, pos4, gate4, w1, w3, w2, g2, x3, final_w)


def moe_block(x3, wm, sh, g2, router_wt, layer, w1, w3, w2, final_w, nb, final_norm, name):
    tokens = x3.shape[1]
    cap = EC_CAPACITY * tokens // N_EXPERTS
    hn, pos, gate = moe_router(x3, wm, sh, router_wt, cap, name + "_router")
    return moe_experts(hn, pos, gate, layer, w1, w3, w2, g2, x3, final_w, cap, nb, final_norm,
                       name + "_experts")


def kernel(x, c, ctx, c_ctx, ada_w, ada_b, norm1_w, norm2_w, ssd_in_w, ssd_conv_w, ssd_conv_b, ssd_dt_bias, ssd_A_log, ssd_D, ssd_norm_w, ssd_out_w, attn_qkv_w, attn_q_norm_w, attn_k_norm_w, attn_out_w, moe_router_w, moe_w1, moe_w3, moe_w2, final_norm_w):
    batch, seq, d = x.shape
    ctx_len = ctx.shape[1]
    depth = ada_w.shape[0]
    assert depth == 2 and d == D_MODEL

    mod_rows = -(-(batch + 1) // SUBLANES) * SUBLANES
    c_all = jnp.zeros((mod_rows, d), F32).at[:batch].set(c).at[batch].set(c_ctx)
    mod = ada_modulation(c_all, ada_w, ada_b)

    def mod_parts(layer):
        parts = jnp.split(mod[layer], 6, axis=-1)
        lat = [p[:batch].reshape(batch, 1, d) for p in parts]
        cx = [p[batch:batch + 1].reshape(1, 1, d) for p in parts]
        return lat, cx

    final_w = final_norm_w.reshape(1, d)
    x2 = x.reshape(batch * seq, d)
    c2 = ctx.reshape(batch * ctx_len, d)

    lat, cx = mod_parts(0)
    n1 = norm1_w[0].reshape(1, 1, d)
    n2 = norm2_w[0].reshape(1, 1, d)
    n_main = SSD_IN_DIM - 2 * SSD_HEADS
    in_w = ssd_in_w[0][:, :n_main].astype(BF16)
    in_w_dt = jnp.pad(ssd_in_w[0][:, n_main:], ((0, 0), (0, LANES - 2 * SSD_HEADS))).astype(BF16)
    proj_l, dtr_l = norm_mod_mm(x2, n1 * (1.0 + lat[1]), lat[0], in_w, seq, 2048, 768, "ssd_in_lat", in_w_dt)
    proj_c, dtr_c = norm_mod_mm(c2, n1 * (1.0 + cx[1]), cx[0], in_w, batch * ctx_len, 1024, 768, "ssd_in_ctx",
                                in_w_dt)
    proj_l3 = proj_l.reshape(batch, seq, n_main)
    proj_c3 = proj_c.reshape(batch, ctx_len, n_main)
    dt_bias = jnp.pad(ssd_dt_bias[0].reshape(1, 2 * SSD_HEADS), ((0, 0), (0, LANES - 2 * SSD_HEADS)))
    a_neg = jnp.pad(-jnp.exp(ssd_A_log[0].astype(F32)).reshape(1, 2 * SSD_HEADS),
                    ((0, 0), (0, LANES - 2 * SSD_HEADS)))
    col_l, row_l = ssd_prep(dtr_l.reshape(batch, seq, LANES), dt_bias, a_neg, "ssd_prep_lat")
    col_c, row_c = ssd_prep(dtr_c.reshape(batch, ctx_len, LANES), dt_bias, a_neg, "ssd_prep_ctx")
    dskip_e = jnp.repeat(ssd_D[0], SSD_HEADDIM).reshape(SSD_GROUPS, 1, SSD_HPG * SSD_HEADDIM)
    y_c, y_l = ssd_scan(proj_c3, proj_l3, ssd_conv_w[0], ssd_conv_b[0].reshape(1, CONV_DIM),
                        col_c, row_c, col_l, row_l, dskip_e)
    out_w = (ssd_norm_w[0].reshape(D_INNER, 1) * ssd_out_w[0]).astype(BF16)
    x2 = ssd_out_mm(y_l.reshape(batch * seq, D_INNER), proj_l, out_w, x2, lat[2], seq, 1024, "ssd_out_lat")
    c2 = ssd_out_mm(y_c.reshape(batch * ctx_len, D_INNER), proj_c, out_w, c2, cx[2], batch * ctx_len, 1024,
                    "ssd_out_ctx")

    rwt = moe_router_w[0].T
    w1, w3, w2 = moe_w1.astype(BF16), moe_w3.astype(BF16), moe_w2.astype(BF16)
    x3 = moe_block(x2.reshape(batch, seq, d), n2 * (1.0 + lat[4]), lat[3], lat[5], rwt, 0, w1, w3, w2,
                   final_w, 1, False, "moe0_lat")
    c3 = moe_block(c2.reshape(batch, ctx_len, d), n2 * (1.0 + cx[4]), cx[3], cx[5], rwt, 0, w1, w3, w2,
                   final_w, math.gcd(batch, 8), False, "moe0_ctx")
    x2 = x3.reshape(batch * seq, d)
    c2 = c3.reshape(batch * ctx_len, d)

    lat, cx = mod_parts(1)
    n1 = norm1_w[1].reshape(1, 1, d)
    n2 = norm2_w[1].reshape(1, 1, d)
    qkv_w = attn_qkv_w[0].astype(BF16)
    qkv_l, = norm_mod_mm(x2, n1 * (1.0 + lat[1]), lat[0], qkv_w, seq, 2048, 768, "qkv_lat")
    qkv_c, = norm_mod_mm(c2, n1 * (1.0 + cx[1]), cx[0], qkv_w, batch * ctx_len, 1024, 768, "qkv_ctx")
    tables = _attn_tables(seq)
    q_l, k_l, v_l = qk_prep(qkv_l, attn_q_norm_w[0], attn_k_norm_w[0], tables, seq, True, 512, "qk_prep_lat")
    _, k_c, v_c = qk_prep(qkv_c, attn_q_norm_w[0], attn_k_norm_w[0], tables, ctx_len, False, ctx_len,
                          "qk_prep_ctx")
    o = attention(q_l, k_l, v_l, k_c, v_c, batch, seq, ctx_len, 256)
    x2 = res_mm(o, attn_out_w[0].astype(BF16), x2, lat[2], seq, 1024, "attn_out")

    rwt = moe_router_w[1].T
    x3 = moe_block(x2.reshape(batch, seq, d), n2 * (1.0 + lat[4]), lat[3], lat[5], rwt, 1, w1, w3, w2,
                   final_w, 1, True, "moe1_lat")
    return x3
```

```python
import functools
import math

import numpy as np
import jax
import jax.numpy as jnp
from jax import lax
from jax.experimental import pallas as pl
from jax.experimental.pallas import tpu as pltpu

F32 = jnp.float32
BF16 = jnp.bfloat16
HIGHEST = lax.Precision.HIGHEST

D_MODEL = 1024
GRID_W = 64
EPS = 1e-6
D_INNER = 2048
SSD_HEADDIM = 64
SSD_HEADS = 32
SSD_GROUPS = 8
SSD_HPG = 4
D_STATE = 128
CONV_W = 5
SSD_CHUNK = 128
CONV_DIM = D_INNER + 2 * SSD_GROUPS * D_STATE
SSD_IN_DIM = 2 * D_INNER + 2 * SSD_GROUPS * D_STATE + 2 * SSD_HEADS
HEAD_DIM = 64
N_Q_HEADS = 16
N_KV_HEADS = 4
Q_PER_KV = 4
D_ATTN = 1024
KV_DIM = N_KV_HEADS * HEAD_DIM
QKV_DIM = D_ATTN + 2 * KV_DIM
ROPE_THETA = 10000.0
N_EXPERTS = 16
EC_CAPACITY = 2

LANES = 128
SUBLANES = 8
VMEM_LIMIT_BYTES = 56 * 1024 * 1024


def _cparams(*sem):
    return pltpu.CompilerParams(dimension_semantics=sem, vmem_limit_bytes=VMEM_LIMIT_BYTES)


LOG2E = math.log2(math.e)


def _sigmoid(x):
    return 1.0 / (1.0 + jnp.exp2(x * (-LOG2E)))


def _silu(x):
    return x * _sigmoid(x)


def _split2(x):
    hi = x.astype(BF16)
    return hi, (x - hi.astype(F32)).astype(BF16)


def _ada_kernel(c_ref, w_ref, b_ref, o_ref):
    a = _silu(c_ref[...])
    o_ref[...] = jnp.dot(a, w_ref[...], precision=HIGHEST, preferred_element_type=F32) + b_ref[...]


def ada_modulation(c_all, ada_w, ada_b):
    depth, d, n = ada_w.shape
    m = c_all.shape[0]
    tn = 1536
    return pl.pallas_call(
        _ada_kernel,
        grid=(depth, n // tn),
        in_specs=[pl.BlockSpec((m, d), lambda l, j: (0, 0)),
                  pl.BlockSpec((None, d, tn), lambda l, j: (l, 0, j)),
                  pl.BlockSpec((None, 1, tn), lambda l, j: (l, 0, j))],
        out_specs=pl.BlockSpec((None, m, tn), lambda l, j: (l, 0, j)),
        out_shape=jax.ShapeDtypeStruct((depth, m, n), F32),
        compiler_params=_cparams("parallel", "parallel"),
        name="ada_modulation",
    )(c_all, ada_w, ada_b.reshape(depth, 1, n))


def _norm_mod(x_ref, wm_ref, sh_ref):
    x = x_ref[...]
    ms = jnp.mean(x * x, axis=-1, keepdims=True)
    return (x * lax.rsqrt(ms + EPS) * wm_ref[...] + sh_ref[...]).astype(BF16)


def _norm_mod_mm_kernel(x_ref, wm_ref, sh_ref, w_ref, o_ref, a_scr):
    @pl.when(pl.program_id(1) == 0)
    def _():
        a_scr[...] = _norm_mod(x_ref, wm_ref, sh_ref)

    o_ref[...] = jnp.dot(a_scr[...], w_ref[...], preferred_element_type=F32).astype(BF16)


def _norm_mod_mm2_kernel(x_ref, wm_ref, sh_ref, w_ref, wf_ref, o_ref, of_ref, a_scr):
    @pl.when(pl.program_id(1) == 0)
    def _():
        a_scr[...] = _norm_mod(x_ref, wm_ref, sh_ref)
        of_ref[...] = jnp.dot(a_scr[...], wf_ref[...], preferred_element_type=F32)

    o_ref[...] = jnp.dot(a_scr[...], w_ref[...], preferred_element_type=F32).astype(BF16)


def norm_mod_mm(x2d, wm, sh, w, rows_per_group, tm, tn, name, w_f32out=None):
    m, k = x2d.shape
    n = w.shape[1]
    tiles_per_group = rows_per_group // tm
    in_specs = [pl.BlockSpec((tm, k), lambda i, j: (i, 0)),
                pl.BlockSpec((None, 1, k), lambda i, j: (i // tiles_per_group, 0, 0)),
                pl.BlockSpec((None, 1, k), lambda i, j: (i // tiles_per_group, 0, 0)),
                pl.BlockSpec((k, tn), lambda i, j: (0, j))]
    out_specs = [pl.BlockSpec((tm, tn), lambda i, j: (i, j))]
    out_shape = [jax.ShapeDtypeStruct((m, n), BF16)]
    args = [x2d, wm, sh, w]
    body = _norm_mod_mm_kernel
    if w_f32out is not None:
        nf = w_f32out.shape[1]
        in_specs.append(pl.BlockSpec((k, nf), lambda i, j: (0, 0)))
        out_specs.append(pl.BlockSpec((tm, nf), lambda i, j: (i, 0)))
        out_shape.append(jax.ShapeDtypeStruct((m, nf), F32))
        args.append(w_f32out)
        body = _norm_mod_mm2_kernel
    return pl.pallas_call(
        body,
        grid=(m // tm, n // tn),
        in_specs=in_specs, out_specs=out_specs, out_shape=out_shape,
        scratch_shapes=[pltpu.VMEM((tm, k), BF16)],
        compiler_params=_cparams("parallel", "arbitrary"),
        name=name,
    )(*args)


def _res_mm_kernel(a_ref, w_ref, res_ref, gate_ref, o_ref):
    acc = jnp.dot(a_ref[...], w_ref[...], preferred_element_type=F32)
    o_ref[...] = res_ref[...] + gate_ref[...] * acc


def res_mm(a, w, res, gate, rows_per_group, tm, name):
    m, k = a.shape
    n = w.shape[1]
    tiles_per_group = rows_per_group // tm
    return pl.pallas_call(
        _res_mm_kernel,
        grid=(m // tm,),
        in_specs=[pl.BlockSpec((tm, k), lambda i: (i, 0)),
                  pl.BlockSpec((k, n), lambda i: (0, 0)),
                  pl.BlockSpec((tm, n), lambda i: (i, 0)),
                  pl.BlockSpec((None, 1, n), lambda i: (i // tiles_per_group, 0, 0))],
        out_specs=pl.BlockSpec((tm, n), lambda i: (i, 0)),
        out_shape=jax.ShapeDtypeStruct((m, n), F32),
        compiler_params=_cparams("parallel"),
        name=name,
    )(a, w, res, gate)


SSD_OUT_SUB_ROWS = 512


def _ssd_out_kernel(y_ref, z_ref, w_ref, res_ref, gate_ref, o_ref):
    for r in range(y_ref.shape[0] // SSD_OUT_SUB_ROWS):
        rows = slice(r * SSD_OUT_SUB_ROWS, (r + 1) * SSD_OUT_SUB_ROWS)
        v = y_ref[rows, :].astype(F32) * _silu(z_ref[rows, :].astype(F32))
        ms = jnp.mean(v * v, axis=-1, keepdims=True)
        acc = jnp.dot(v.astype(BF16), w_ref[...], preferred_element_type=F32)
        o_ref[rows, :] = res_ref[rows, :] + gate_ref[...] * (lax.rsqrt(ms + EPS) * acc)


def ssd_out_mm(y, proj, w_normed, res, gate, rows_per_group, tm, name):
    m, k = y.shape
    n = w_normed.shape[1]
    tiles_per_group = rows_per_group // tm
    return pl.pallas_call(
        _ssd_out_kernel,
        grid=(m // tm,),
        in_specs=[pl.BlockSpec((tm, k), lambda i: (i, 0)),
                  pl.BlockSpec((tm, k), lambda i: (i, 0)),
                  pl.BlockSpec((k, n), lambda i: (0, 0)),
                  pl.BlockSpec((tm, n), lambda i: (i, 0)),
                  pl.BlockSpec((None, 1, n), lambda i: (i // tiles_per_group, 0, 0))],
        out_specs=pl.BlockSpec((tm, n), lambda i: (i, 0)),
        out_shape=jax.ShapeDtypeStruct((m, n), F32),
        compiler_params=_cparams("parallel"),
        name=name,
    )(y, proj, w_normed, res, gate)


def _ssd_prep_kernel(dtr_ref, bias_ref, a_ref, scol_ref, srow_ref, col_ref, row_ref):
    seq = dtr_ref.shape[0]
    x = dtr_ref[...] + bias_ref[...]
    dt = jnp.maximum(x, 0.0) + jnp.log1p(jnp.exp(-jnp.abs(x)))
    a = dt * a_ref[...]
    t_i = lax.broadcasted_iota(jnp.int32, (SSD_CHUNK, SSD_CHUNK), 0)
    s_i = lax.broadcasted_iota(jnp.int32, (SSD_CHUNK, SSD_CHUNK), 1)
    tri_lo = (s_i <= t_i).astype(F32)
    tri_up = (s_i >= t_i).astype(F32)
    fwd_lane = lax.broadcasted_iota(jnp.int32, (SSD_CHUNK, LANES), 1) < SSD_HEADS
    chunks = []
    for c in range(seq // SSD_CHUNK):
        ac = a[c * SSD_CHUNK:(c + 1) * SSD_CHUNK]
        cf = jnp.dot(tri_lo, ac, precision=HIGHEST, preferred_element_type=F32)
        cb = jnp.dot(tri_up, ac, precision=HIGHEST, preferred_element_type=F32)
        chunks.append(jnp.where(fwd_lane, cf, cb))
    cum = jnp.concatenate(chunks, axis=0)
    both = jnp.concatenate([dt, cum], axis=1)
    hi = both.astype(BF16)
    rest = both - hi.astype(F32)
    mid = rest.astype(BF16)
    lo = (rest - mid.astype(F32)).astype(BF16)
    scol = scol_ref[...]
    col_ref[...] = (jnp.dot(hi, scol, preferred_element_type=F32)
                    + jnp.dot(mid, scol, preferred_element_type=F32)
                    + jnp.dot(lo, scol, preferred_element_type=F32))
    row_ref[...] = lax.dot_general(srow_ref[...], cum, (((1,), (1,)), ((), ())),
                                   precision=HIGHEST, preferred_element_type=F32)


def _ssd_select_tables():
    scol = np.zeros((2 * LANES, SSD_GROUPS * LANES), np.float32)
    srow = np.zeros((SSD_GROUPS * SUBLANES, LANES), np.float32)
    for g in range(SSD_GROUPS):
        for d in range(2):
            for r in range(SSD_HPG):
                lane = d * SSD_HEADS + SSD_HPG * g + r
                scol[lane, g * LANES + d * SSD_HPG + r] = 1.0
                scol[LANES + lane, g * LANES + 2 * SSD_HPG + d * SSD_HPG + r] = 1.0
                srow[g * SUBLANES + d * SSD_HPG + r, lane] = 1.0
    return jnp.asarray(scol, dtype=BF16), jnp.asarray(srow)


def ssd_prep(dt_raw3, dt_bias, a_neg, name):
    b, seq, _ = dt_raw3.shape
    scol, srow = _ssd_select_tables()
    return pl.pallas_call(
        _ssd_prep_kernel,
        grid=(b,),
        in_specs=[pl.BlockSpec((None, seq, LANES), lambda i: (i, 0, 0)),
                  pl.BlockSpec((1, LANES), lambda i: (0, 0)),
                  pl.BlockSpec((1, LANES), lambda i: (0, 0)),
                  pl.BlockSpec(scol.shape, lambda i: (0, 0)),
                  pl.BlockSpec(srow.shape, lambda i: (0, 0))],
        out_specs=[pl.BlockSpec((None, seq, SSD_GROUPS * LANES), lambda i: (i, 0, 0)),
                   pl.BlockSpec((None, SSD_GROUPS * SUBLANES, seq), lambda i: (i, 0, 0))],
        out_shape=[jax.ShapeDtypeStruct((b, seq, SSD_GROUPS * LANES), F32),
                   jax.ShapeDtypeStruct((b, SSD_GROUPS * SUBLANES, seq), F32)],
        compiler_params=_cparams("parallel"),
        name=name,
    )(dt_raw3, dt_bias, a_neg, scol, srow)


CONV_HALO = 2 * SUBLANES
CONV_TAPS_OFF_CENTRE = (0, 1, 3, 4)
SSD_UNROLL = 8
SSD_PREP_UNROLL = 4


def _ssd_scan_kernel(xc_ref, bc_ref, cc_ref, xl_ref, bl_ref, cl_ref,
                     cwx_ref, cwb_ref, cwc_ref, cbx_ref, cbb_ref, cbc_ref,
                     colc_ref, rowc_ref, coll_ref, rowl_ref, dsk_ref,
                     yc_ref, yl_ref,
                     pad_scr, xs_scr, bt_scr, cs_scr, y_scr, xdt_scr, xdw_scr, ecum_scr, seg_scr, ed_scr,
                     hf_scr, hb_scr):
    q = SSD_CHUNK
    width = SSD_HPG * SSD_HEADDIM
    win_rows = q + 2 * CONV_HALO
    dsk = dsk_ref[...]

    r_i = lax.broadcasted_iota(jnp.int32, (len(CONV_TAPS_OFF_CENTRE) * q, win_rows), 0)
    j_i = lax.broadcasted_iota(jnp.int32, (len(CONV_TAPS_OFF_CENTRE) * q, win_rows), 1)
    tap_of_row = jnp.zeros_like(r_i)
    for n, k in enumerate(CONV_TAPS_OFF_CENTRE):
        tap_of_row = jnp.where(r_i // q == n, k, tap_of_row)
    shift_mat = jnp.where(j_i == r_i % q + CONV_HALO - CONV_W // 2 + tap_of_row, 1.0, 0.0).astype(BF16)

    def conv_silu(parts, w, bias, seq, store):
        zeros = jnp.zeros((CONV_HALO, width), BF16)
        pad_scr[0:CONV_HALO, :] = zeros
        pad_scr[CONV_HALO + seq:2 * CONV_HALO + seq, :] = zeros

        def copy_body(c, carry):
            base = pl.multiple_of(c * q, q)
            dst = pl.multiple_of(base + CONV_HALO, CONV_HALO)
            for raw_ref, lane0 in parts:
                pad_scr[pl.ds(dst, q), lane0:lane0 + raw_ref.shape[1]] = raw_ref[pl.ds(base, q), :]
            return carry

        lax.fori_loop(0, seq // q, copy_body, 0)

        def conv_body(c, carry):
            base = pl.multiple_of(c * q, q)
            win = pad_scr[pl.ds(base, win_rows), :]
            shifted = jnp.dot(shift_mat, win, preferred_element_type=F32)
            acc = bias + win[CONV_HALO:CONV_HALO + q].astype(F32) * w[CONV_W // 2:CONV_W // 2 + 1]
            for n, k in enumerate(CONV_TAPS_OFF_CENTRE):
                acc = acc + shifted[n * q:(n + 1) * q] * w[k:k + 1]
            store(base, _silu(acc))
            return carry

        lax.fori_loop(0, seq // q, conv_body, 0, unroll=min(SSD_UNROLL, seq // q))

    def x_store(base, v):
        xs_scr[pl.ds(base, q), :] = v
        y_scr[pl.ds(base, q), :] = dsk * v

    def bc_store(base, v):
        bt_scr[:, pl.ds(base, q)] = v[:, 0:D_STATE].T.astype(BF16)
        cs_scr[pl.ds(base, q), :] = v[:, D_STATE:2 * D_STATE].astype(BF16)

    w_bc = jnp.concatenate([cwb_ref[...], cwc_ref[...]], axis=1)
    b_bc = jnp.concatenate([cbb_ref[...], cbc_ref[...]], axis=1)

    t_i = lax.broadcasted_iota(jnp.int32, (q, q), 0)
    s_i = lax.broadcasted_iota(jnp.int32, (q, q), 1)
    head_of_lane = lax.broadcasted_iota(jnp.int32, (q, width), 1) // SSD_HEADDIM

    def per_head(parts):
        out = parts[SSD_HPG - 1]
        for r in range(SSD_HPG - 2, -1, -1):
            out = jnp.where(head_of_lane == r, parts[r], out)
        return out

    src_lane = lax.broadcasted_iota(jnp.int32, (LANES, 2 * width), 0)
    dst_head = lax.broadcasted_iota(jnp.int32, (LANES, 2 * width), 1) // SSD_HEADDIM
    dt_spread = jnp.where(src_lane == dst_head, 1.0, 0.0).astype(BF16)

    first_head_lanes = lax.broadcasted_iota(jnp.int32, (q, q), 1) < SSD_HEADDIM

    def per_head_lanes(parts):
        return jnp.concatenate([jnp.where(first_head_lanes, parts[0], parts[1]),
                                jnp.where(first_head_lanes, parts[2], parts[3])], axis=1)

    def run_seq(x_ref, b_ref, c_ref, col_ref, row_ref, y_ref):
        seq = x_ref.shape[0]
        nchunks = seq // q
        conv_silu([(x_ref, 0)], cwx_ref[...], cbx_ref[...], seq, x_store)
        conv_silu([(b_ref, 0), (c_ref, D_STATE)], w_bc, b_bc, seq, bc_store)

        def decay_terms(c, d, cp, dt_e):
            mask = (s_i <= t_i) if d == 0 else (s_i >= t_i)
            edge = q - 1 if d == 0 else 0
            base = pl.multiple_of(c * q, q)
            rp = row_ref[:, pl.ds(base, q)]
            cum_b = [jnp.broadcast_to(cp[:, 2 * SSD_HPG + SSD_HPG * d + r:2 * SSD_HPG + SSD_HPG * d + r + 1],
                                      (q, q)) for r in range(SSD_HPG)]
            cum_e = per_head_lanes(cum_b)
            xdt = xs_scr[pl.ds(base, q), :] * dt_e
            cum_edge = cum_e[edge:edge + 1, :]
            xdt_scr[d, pl.ds(base, q), :] = xdt.astype(BF16)
            xdw_scr[d, pl.ds(base, q), :] = (xdt * jnp.exp2(cum_edge - cum_e)).astype(BF16)
            ecum_scr[d, pl.ds(base, q), :] = jnp.exp2(cum_e)
            ed_scr[d, pl.ds(pl.multiple_of(c * SUBLANES, SUBLANES), SUBLANES), :] = jnp.broadcast_to(
                jnp.exp2(cum_edge), (SUBLANES, width))
            for r in range(SSD_HPG):
                row = rp[SSD_HPG * d + r:SSD_HPG * d + r + 1, :]
                seg = jnp.exp2(jnp.where(mask, cum_b[r] - row, -1e30))
                seg_scr[SSD_HPG * d + r, pl.ds(base, q), :] = seg.astype(BF16)

        def prep_body(c, carry):
            cp = col_ref[pl.ds(pl.multiple_of(c * q, q), q), :]
            cp_hi, cp_lo = _split2(cp)
            dt_all = (jnp.dot(cp_hi, dt_spread, preferred_element_type=F32)
                      + jnp.dot(cp_lo, dt_spread, preferred_element_type=F32))
            decay_terms(c, 0, cp, dt_all[:, 0:width])
            decay_terms(c, 1, cp, dt_all[:, width:2 * width])
            return carry

        lax.fori_loop(0, nchunks, prep_body, 0, unroll=min(SSD_PREP_UNROLL, nchunks))

        def chunk(c, d, h_scr):
            base = pl.multiple_of(c * q, q)
            bq_t = bt_scr[:, pl.ds(base, q)]
            cq = cs_scr[pl.ds(base, q), :]
            cb = jnp.dot(cq, bq_t, preferred_element_type=F32).astype(BF16)
            ms = [cb * seg_scr[SSD_HPG * d + r, pl.ds(base, q), :] for r in range(SSD_HPG)]
            stacked = jnp.dot(jnp.concatenate(ms, axis=0), xdt_scr[d, pl.ds(base, q), :],
                              preferred_element_type=F32)
            y = per_head([stacked[r * q:(r + 1) * q] for r in range(SSD_HPG)])
            h_t = h_scr[...]
            y = y + jnp.dot(cq, h_t.astype(BF16), preferred_element_type=F32) * ecum_scr[d, pl.ds(base, q), :]
            edge_decay = ed_scr[d, pl.ds(pl.multiple_of(c * SUBLANES, SUBLANES), SUBLANES), :][0:1]
            h_scr[...] = h_t * edge_decay + jnp.dot(bq_t, xdw_scr[d, pl.ds(base, q), :],
                                                    preferred_element_type=F32)
            y_scr[pl.ds(base, q), :] += y

        def scan_body(i, carry):
            chunk(i, 0, hf_scr)
            chunk(nchunks - 1 - i, 1, hb_scr)
            return carry

        lax.fori_loop(0, nchunks, scan_body, 0, unroll=min(SSD_UNROLL, nchunks))

        def out_body(c, carry):
            base = pl.multiple_of(c * q, q)
            y_ref[pl.ds(base, q), :] = y_scr[pl.ds(base, q), :].astype(BF16)
            return carry

        lax.fori_loop(0, nchunks, out_body, 0)

    hf_scr[...] = jnp.zeros(hf_scr.shape, F32)
    hb_scr[...] = jnp.zeros(hb_scr.shape, F32)
    run_seq(xc_ref, bc_ref, cc_ref, colc_ref, rowc_ref, yc_ref)
    run_seq(xl_ref, bl_ref, cl_ref, coll_ref, rowl_ref, yl_ref)


def ssd_scan(proj_c, proj_l, conv_w, conv_b, col_c, row_c, col_l, row_l, dskip_e):
    b, lc, _ = proj_c.shape
    ll = proj_l.shape[1]
    width = SSD_HPG * SSD_HEADDIM
    x0 = D_INNER // width
    b0 = (2 * D_INNER) // D_STATE
    c0 = b0 + SSD_GROUPS
    wx0, wb0, wc0 = 0, D_INNER // D_STATE, D_INNER // D_STATE + SSD_GROUPS

    def seq_specs(seq):
        return [pl.BlockSpec((None, seq, width), lambda i, g: (i, 0, x0 + g)),
                pl.BlockSpec((None, seq, D_STATE), lambda i, g: (i, 0, b0 + g)),
                pl.BlockSpec((None, seq, D_STATE), lambda i, g: (i, 0, c0 + g))]

    def par_specs(rows):
        return [pl.BlockSpec((rows, width), lambda i, g: (0, wx0 + g)),
                pl.BlockSpec((rows, D_STATE), lambda i, g: (0, wb0 + g)),
                pl.BlockSpec((rows, D_STATE), lambda i, g: (0, wc0 + g))]

    def pack_specs(seq):
        return [pl.BlockSpec((None, seq, LANES), lambda i, g: (i, 0, g)),
                pl.BlockSpec((None, SUBLANES, seq), lambda i, g: (i, g, 0))]

    return pl.pallas_call(
        _ssd_scan_kernel,
        grid=(b, SSD_GROUPS),
        in_specs=(seq_specs(lc) + seq_specs(ll) + par_specs(CONV_W) + par_specs(1)
                  + pack_specs(lc) + pack_specs(ll)
                  + [pl.BlockSpec((None, 1, width), lambda i, g: (g, 0, 0))]),
        out_specs=[pl.BlockSpec((None, lc, width), lambda i, g: (i, 0, g)),
                   pl.BlockSpec((None, ll, width), lambda i, g: (i, 0, g))],
        out_shape=[jax.ShapeDtypeStruct((b, lc, D_INNER), BF16),
                   jax.ShapeDtypeStruct((b, ll, D_INNER), BF16)],
        scratch_shapes=[pltpu.VMEM((ll + 2 * CONV_HALO, width), BF16),
                        pltpu.VMEM((ll, width), F32),
                        pltpu.VMEM((D_STATE, ll), BF16),
                        pltpu.VMEM((ll, D_STATE), BF16),
                        pltpu.VMEM((ll, width), F32),
                        pltpu.VMEM((2, ll, width), BF16),
                        pltpu.VMEM((2, ll, width), BF16),
                        pltpu.VMEM((2, ll, width), F32),
                        pltpu.VMEM((2 * SSD_HPG, ll, SSD_CHUNK), BF16),
                        pltpu.VMEM((2, ll // SSD_CHUNK * SUBLANES, width), F32),
                        pltpu.VMEM((D_STATE, width), F32), pltpu.VMEM((D_STATE, width), F32)],
        compiler_params=_cparams("parallel", "parallel"),
        name="ssd_scan",
    )(proj_c, proj_c, proj_c, proj_l, proj_l, proj_l,
      conv_w, conv_w, conv_w, conv_b, conv_b, conv_b,
      col_c, row_c, col_l, row_l, dskip_e)


def _qk_prep_kernel(qkv_ref, gmat_ref, qw_ref, kw_ref, rope_c_ref, rope_a_ref, rope_b_ref, rep_ref,
                    q_ref, k_ref, v_ref, *, use_rope):
    blk = Q_PER_KV * HEAD_DIM
    gmat = gmat_ref[...]

    def norm_rope(x, w):
        sq_hi, sq_lo = _split2(x * x)
        ms = (jnp.dot(sq_hi, gmat, preferred_element_type=F32)
              + jnp.dot(sq_lo, gmat, preferred_element_type=F32))
        xn = x * lax.rsqrt(ms + EPS) * w
        if use_rope:
            half = HEAD_DIM // 2
            xn = (xn * rope_c_ref[...] + pltpu.roll(xn, blk - half, 1) * rope_a_ref[...]
                  + pltpu.roll(xn, half, 1) * rope_b_ref[...])
        return xn

    scale = math.log2(math.e) / math.sqrt(HEAD_DIM)
    for j in range(D_ATTN // blk):
        xq = norm_rope(qkv_ref[:, j * blk:(j + 1) * blk].astype(F32), qw_ref[...])
        q_ref[:, j * blk:(j + 1) * blk] = (xq * scale).astype(BF16)
    xk = norm_rope(qkv_ref[:, D_ATTN:D_ATTN + KV_DIM].astype(F32), kw_ref[...]).astype(BF16)
    xv = qkv_ref[:, D_ATTN + KV_DIM:D_ATTN + 2 * KV_DIM]
    for kv in range(N_KV_HEADS):
        rep = rep_ref[kv]
        k_ref[:, kv * blk:(kv + 1) * blk] = jnp.dot(xk, rep, preferred_element_type=F32).astype(BF16)
        v_ref[:, kv * blk:(kv + 1) * blk] = jnp.dot(xv, rep, preferred_element_type=F32).astype(BF16)


def _attn_tables(seq):
    rows = seq // GRID_W
    row = np.repeat(np.arange(rows), GRID_W).astype(np.float32)
    col = np.tile(np.arange(GRID_W), rows).astype(np.float32)
    axis_dim = HEAD_DIM // 2
    inv_freq = jnp.asarray(ROPE_THETA, F32) ** (-jnp.arange(0, axis_dim, 2, dtype=F32) / axis_dim)
    ang = jnp.concatenate([jnp.asarray(row)[:, None] * inv_freq, jnp.asarray(col)[:, None] * inv_freq], axis=-1)
    cos, sin = jnp.cos(ang), jnp.sin(ang)
    zero = jnp.zeros_like(sin)
    reps = Q_PER_KV
    rope_c = jnp.tile(jnp.concatenate([cos, cos], axis=-1), (1, reps))
    rope_a = jnp.tile(jnp.concatenate([-sin, zero], axis=-1), (1, reps))
    rope_b = jnp.tile(jnp.concatenate([zero, sin], axis=-1), (1, reps))
    return rope_c, rope_a, rope_b


def _attn_consts():
    blk = Q_PER_KV * HEAD_DIM
    head = np.arange(blk) // HEAD_DIM
    gmat = (head[:, None] == head[None, :]).astype(np.float32) / HEAD_DIM
    rep = np.zeros((N_KV_HEADS, KV_DIM, blk), np.float32)
    for kv in range(N_KV_HEADS):
        for j in range(blk):
            rep[kv, kv * HEAD_DIM + j % HEAD_DIM, j] = 1.0
    return jnp.asarray(gmat, dtype=BF16), jnp.asarray(rep, dtype=BF16)


def qk_prep(qkv, q_norm_w, k_norm_w, tables, seq, use_rope, tl, name):
    m = qkv.shape[0]
    blk = Q_PER_KV * HEAD_DIM
    gmat, rep = _attn_consts()
    qw = jnp.tile(q_norm_w.reshape(1, HEAD_DIM), (1, Q_PER_KV))
    kw = jnp.tile(k_norm_w.reshape(1, HEAD_DIM), (1, N_KV_HEADS))
    tiles_per_seq = seq // tl
    tab_spec = pl.BlockSpec((tl, blk), lambda i: (i % tiles_per_seq, 0))
    out_w = N_KV_HEADS * blk
    return pl.pallas_call(
        functools.partial(_qk_prep_kernel, use_rope=use_rope),
        grid=(m // tl,),
        in_specs=[pl.BlockSpec((tl, QKV_DIM), lambda i: (i, 0)),
                  pl.BlockSpec((blk, blk), lambda i: (0, 0)),
                  pl.BlockSpec((1, blk), lambda i: (0, 0)),
                  pl.BlockSpec((1, blk), lambda i: (0, 0)),
                  tab_spec, tab_spec, tab_spec,
                  pl.BlockSpec(rep.shape, lambda i: (0, 0, 0))],
        out_specs=[pl.BlockSpec((tl, D_ATTN), lambda i: (i, 0)),
                   pl.BlockSpec((tl, out_w), lambda i: (i, 0)),
                   pl.BlockSpec((tl, out_w), lambda i: (i, 0))],
        out_shape=[jax.ShapeDtypeStruct((m, D_ATTN), BF16),
                   jax.ShapeDtypeStruct((m, out_w), BF16),
                   jax.ShapeDtypeStruct((m, out_w), BF16)],
        compiler_params=_cparams("parallel"),
        name=name,
    )(qkv, gmat, qw, kw, *tables, rep)


def _attn_kernel(q_ref, kl_ref, vl_ref, kc_ref, vc_ref, o_ref):
    tq = q_ref.shape[0]
    blk = Q_PER_KV * HEAD_DIM
    head_of_lane = lax.broadcasted_iota(jnp.int32, (tq, blk), 1) // HEAD_DIM
    nt = (((1,), (1,)), ((), ()))
    for kv in range(N_KV_HEADS):
        cols = slice(kv * blk, (kv + 1) * blk)
        q4 = q_ref[:, cols]
        zero = jnp.zeros_like(q4)
        qs = jnp.concatenate([jnp.where(head_of_lane == g, q4, zero) for g in range(Q_PER_KV)], axis=0)
        half = kl_ref.shape[0] // 2
        s_a = lax.dot_general(qs, kl_ref[0:half, cols], nt, preferred_element_type=F32)
        s_b = lax.dot_general(qs, kl_ref[half:2 * half, cols], nt, preferred_element_type=F32)
        s_c = lax.dot_general(qs, kc_ref[:, cols], nt, preferred_element_type=F32)
        m = jnp.maximum(jnp.maximum(jnp.max(s_a, axis=-1, keepdims=True), jnp.max(s_b, axis=-1, keepdims=True)),
                        jnp.max(s_c, axis=-1, keepdims=True))
        p_a = jnp.exp2(s_a - m)
        p_b = jnp.exp2(s_b - m)
        p_c = jnp.exp2(s_c - m)
        denom = (jnp.sum(p_a, axis=-1, keepdims=True) + jnp.sum(p_b, axis=-1, keepdims=True)
                 + jnp.sum(p_c, axis=-1, keepdims=True))
        rows = Q_PER_KV * tq // 2
        o_parts = []
        for h in range(2):
            rs = slice(h * rows, (h + 1) * rows)
            o_parts.append(
                jnp.dot(p_a[rs].astype(BF16), vl_ref[0:half, cols], preferred_element_type=F32)
                + jnp.dot(p_b[rs].astype(BF16), vl_ref[half:2 * half, cols], preferred_element_type=F32)
                + jnp.dot(p_c[rs].astype(BF16), vc_ref[:, cols], preferred_element_type=F32))
        o = jnp.concatenate(o_parts, axis=0) / denom
        out = o[(Q_PER_KV - 1) * tq:Q_PER_KV * tq]
        for g in range(Q_PER_KV - 2, -1, -1):
            out = jnp.where(head_of_lane == g, o[g * tq:(g + 1) * tq], out)
        o_ref[:, cols] = out.astype(BF16)


def attention(q, k_lat, v_lat, k_ctx, v_ctx, batch, seq, ctx_len, tq):
    width = q.shape[1]
    nq = seq // tq
    return pl.pallas_call(
        _attn_kernel,
        grid=(batch, nq),
        in_specs=[pl.BlockSpec((tq, width), lambda b, i: (b * nq + i, 0)),
                  pl.BlockSpec((seq, width), lambda b, i: (b, 0)),
                  pl.BlockSpec((seq, width), lambda b, i: (b, 0)),
                  pl.BlockSpec((ctx_len, width), lambda b, i: (b, 0)),
                  pl.BlockSpec((ctx_len, width), lambda b, i: (b, 0))],
        out_specs=pl.BlockSpec((tq, width), lambda b, i: (b * nq + i, 0)),
        out_shape=jax.ShapeDtypeStruct((batch * seq, width), BF16),
        compiler_params=_cparams("parallel", "parallel"),
        name="attention",
    )(q, k_lat, v_lat, k_ctx, v_ctx)


def _router_kernel(x_ref, wm_ref, sh_ref, rw_ref, hn_ref, pos_ref, gate_ref, *, cap):
    tokens = x_ref.shape[0]
    x = x_ref[...]
    ms = jnp.mean(x * x, axis=-1, keepdims=True)
    hn = x * lax.rsqrt(ms + EPS) * wm_ref[...] + sh_ref[...]
    hn_hi, hn_lo = _split2(hn)
    hn_ref[...] = hn_hi
    rw_hi, rw_lo = _split2(rw_ref[...])
    nt = (((1,), (1,)), ((), ()))
    logits = (lax.dot_general(rw_hi, hn_hi, nt, preferred_element_type=F32)
              + lax.dot_general(rw_hi, hn_lo, nt, preferred_element_type=F32)
              + lax.dot_general(rw_lo, hn_hi, nt, preferred_element_type=F32))
    e = jnp.exp(logits - jnp.max(logits, axis=0, keepdims=True))
    aff = e / jnp.sum(e, axis=0, keepdims=True)
    gate_ref[...] = aff
    bits = pltpu.bitcast(aff, jnp.int32)

    def count(flags):
        return jnp.sum(jnp.where(flags, 1.0, 0.0), axis=1, keepdims=True)

    def search(i, thr):
        cand = thr | jnp.left_shift(jnp.int32(1), 30 - i)
        return jnp.where(count(bits >= cand) >= cap, cand, thr)

    thr = lax.fori_loop(0, 31, search, jnp.zeros((N_EXPERTS, 1), jnp.int32))
    above = bits > thr
    tied = bits == thr
    need = cap - count(above)

    t_i = lax.broadcasted_iota(jnp.int32, (LANES, LANES), 0)
    s_i = lax.broadcasted_iota(jnp.int32, (LANES, LANES), 1)
    tri = (t_i <= s_i).astype(BF16)

    def lane_cumsum(flags):
        carry = jnp.zeros((N_EXPERTS, 1), F32)
        parts = []
        for c in range(tokens // LANES):
            part = jnp.dot(flags[:, c * LANES:(c + 1) * LANES].astype(BF16), tri,
                           preferred_element_type=F32) + carry
            parts.append(part)
            carry = part[:, LANES - 1:LANES]
        return jnp.concatenate(parts, axis=1)

    tie_rank = lane_cumsum(jnp.where(tied, 1.0, 0.0))
    sel = above | (tied & (tie_rank <= need))
    slot = lane_cumsum(jnp.where(sel, 1.0, 0.0)) - 1.0
    pos_ref[...] = jnp.where(sel, slot, -1.0).astype(jnp.int32)


def moe_router(x3, wm, sh, router_wt, cap, name):
    b, tokens, d = x3.shape
    per_batch = wm.shape[0] > 1
    mod_spec = pl.BlockSpec((None, 1, d), (lambda i: (i, 0, 0)) if per_batch else (lambda i: (0, 0, 0)))
    return pl.pallas_call(
        functools.partial(_router_kernel, cap=cap),
        grid=(b,),
        in_specs=[pl.BlockSpec((None, tokens, d), lambda i: (i, 0, 0)),
                  mod_spec, mod_spec,
                  pl.BlockSpec((N_EXPERTS, d), lambda i: (0, 0))],
        out_specs=[pl.BlockSpec((None, tokens, d), lambda i: (i, 0, 0)),
                   pl.BlockSpec((None, N_EXPERTS, tokens), lambda i: (i, 0, 0)),
                   pl.BlockSpec((None, N_EXPERTS, tokens), lambda i: (i, 0, 0))],
        out_shape=[jax.ShapeDtypeStruct((b, tokens, d), BF16),
                   jax.ShapeDtypeStruct((b, N_EXPERTS, tokens), jnp.int32),
                   jax.ShapeDtypeStruct((b, N_EXPERTS, tokens), F32)],
        compiler_params=_cparams("parallel"),
        name=name,
    )(x3, wm, sh, router_wt)


def _expert_kernel(hn_ref, pos_ref, gate_ref, w1_ref, w3_ref, w2_ref, g2_ref, x_ref, fw_ref, o_ref,
                   *, cap, final_norm):
    nb, tokens, _ = hn_ref.shape
    e = pl.program_id(1)

    @pl.when(e == 0)
    def _():
        o_ref[...] = x_ref[...]

    slot_i = lax.broadcasted_iota(jnp.int32, (cap, tokens), 0)
    onehots, gathered, gates = [], [], []
    for n in range(nb):
        hit = slot_i == pos_ref[n]
        onehot = jnp.where(hit, 1.0, 0.0).astype(BF16)
        onehots.append(onehot)
        gates.append(jnp.sum(jnp.where(hit, gate_ref[n], 0.0), axis=1, keepdims=True))
        gathered.append(jnp.dot(onehot, hn_ref[n], preferred_element_type=F32).astype(BF16))
    xg = jnp.concatenate(gathered, axis=0) if nb > 1 else gathered[0]
    a = jnp.dot(xg, w1_ref[...], preferred_element_type=F32)
    g = jnp.dot(xg, w3_ref[...], preferred_element_type=F32)
    y = jnp.dot((_silu(a) * g).astype(BF16), w2_ref[...], preferred_element_type=F32)
    for n in range(nb):
        yn = (y[n * cap:(n + 1) * cap] * gates[n] * g2_ref[n]).astype(BF16)
        o_ref[n] += lax.dot_general(onehots[n], yn, (((0,), (0,)), ((), ())),
                                    preferred_element_type=F32)

    if final_norm:
        @pl.when(e == pl.num_programs(1) - 1)
        def _():
            for n in range(nb):
                v = o_ref[n]
                ms = jnp.mean(v * v, axis=-1, keepdims=True)
                o_ref[n] = v * lax.rsqrt(ms + EPS) * fw_ref[...]


def moe_experts(hn, pos, gate, layer, w1, w3, w2, g2, x3, final_w, cap, nb, final_norm, name):
    b, tokens, d = hn.shape
    g2 = jnp.broadcast_to(g2, (b, 1, d))
    g2_spec = pl.BlockSpec((nb, 1, d), lambda i, e: (i, 0, 0))
    pos4 = pos.reshape(b, N_EXPERTS, 1, tokens)
    gate4 = gate.reshape(b, N_EXPERTS, 1, tokens)
    sel_spec = pl.BlockSpec((nb, None, 1, tokens), lambda i, e: (i, e, 0, 0))
    w_spec = pl.BlockSpec((None, None, d, d), lambda i, e: (layer, e, 0, 0))
    tok_spec = pl.BlockSpec((nb, tokens, d), lambda i, e: (i, 0, 0))
    acc_spec = pl.BlockSpec((nb, tokens, d), lambda i, e: (i, 0, 0), pipeline_mode=pl.Buffered(1))
    return pl.pallas_call(
        functools.partial(_expert_kernel, cap=cap, final_norm=final_norm),
        grid=(b // nb, N_EXPERTS),
        in_specs=[tok_spec, sel_spec, sel_spec, w_spec, w_spec, w_spec, g2_spec, tok_spec,
                  pl.BlockSpec((1, d), lambda i, e: (0, 0))],
        out_specs=acc_spec,
        out_shape=jax.ShapeDtypeStruct((b, tokens, d), F32),
        compiler_params=_cparams("parallel", "arbitrary"),
        name=name,
    )(hn, pos4, gate4, w1, w3, w2, g2, x3, final_w)


def moe_block(x3, wm, sh, g2, router_wt, layer, w1, w3, w2, final_w, nb, final_norm, name):
    tokens = x3.shape[1]
    cap = EC_CAPACITY * tokens // N_EXPERTS
    hn, pos, gate = moe_router(x3, wm, sh, router_wt, cap, name + "_router")
    return moe_experts(hn, pos, gate, layer, w1, w3, w2, g2, x3, final_w, cap, nb, final_norm,
                       name + "_experts")


def kernel(x, c, ctx, c_ctx, ada_w, ada_b, norm1_w, norm2_w, ssd_in_w, ssd_conv_w, ssd_conv_b, ssd_dt_bias, ssd_A_log, ssd_D, ssd_norm_w, ssd_out_w, attn_qkv_w, attn_q_norm_w, attn_k_norm_w, attn_out_w, moe_router_w, moe_w1, moe_w3, moe_w2, final_norm_w):
    batch, seq, d = x.shape
    ctx_len = ctx.shape[1]
    depth = ada_w.shape[0]
    assert depth == 2 and d == D_MODEL

    mod_rows = -(-(batch + 1) // SUBLANES) * SUBLANES
    c_all = jnp.zeros((mod_rows, d), F32).at[:batch].set(c).at[batch].set(c_ctx)
    mod = ada_modulation(c_all, ada_w, ada_b)

    def mod_parts(layer):
        parts = jnp.split(mod[layer], 6, axis=-1)
        lat = [p[:batch].reshape(batch, 1, d) for p in parts]
        cx = [p[batch:batch + 1].reshape(1, 1, d) for p in parts]
        return lat, cx

    final_w = final_norm_w.reshape(1, d)
    x2 = x.reshape(batch * seq, d)
    c2 = ctx.reshape(batch * ctx_len, d)

    lat, cx = mod_parts(0)
    n1 = norm1_w[0].reshape(1, 1, d)
    n2 = norm2_w[0].reshape(1, 1, d)
    n_main = SSD_IN_DIM - 2 * SSD_HEADS
    in_w = ssd_in_w[0][:, :n_main].astype(BF16)
    in_w_dt = jnp.pad(ssd_in_w[0][:, n_main:], ((0, 0), (0, LANES - 2 * SSD_HEADS))).astype(BF16)
    proj_l, dtr_l = norm_mod_mm(x2, n1 * (1.0 + lat[1]), lat[0], in_w, seq, 2048, 768, "ssd_in_lat", in_w_dt)
    proj_c, dtr_c = norm_mod_mm(c2, n1 * (1.0 + cx[1]), cx[0], in_w, batch * ctx_len, 1024, 768, "ssd_in_ctx",
                                in_w_dt)
    proj_l3 = proj_l.reshape(batch, seq, n_main)
    proj_c3 = proj_c.reshape(batch, ctx_len, n_main)
    dt_bias = jnp.pad(ssd_dt_bias[0].reshape(1, 2 * SSD_HEADS), ((0, 0), (0, LANES - 2 * SSD_HEADS)))
    a_neg = jnp.pad((-LOG2E * jnp.exp(ssd_A_log[0].astype(F32))).reshape(1, 2 * SSD_HEADS),
                    ((0, 0), (0, LANES - 2 * SSD_HEADS)))
    col_l, row_l = ssd_prep(dtr_l.reshape(batch, seq, LANES), dt_bias, a_neg, "ssd_prep_lat")
    col_c, row_c = ssd_prep(dtr_c.reshape(batch, ctx_len, LANES), dt_bias, a_neg, "ssd_prep_ctx")
    dskip_e = jnp.repeat(ssd_D[0], SSD_HEADDIM).reshape(SSD_GROUPS, 1, SSD_HPG * SSD_HEADDIM)
    y_c, y_l = ssd_scan(proj_c3, proj_l3, ssd_conv_w[0], ssd_conv_b[0].reshape(1, CONV_DIM),
                        col_c, row_c, col_l, row_l, dskip_e)
    out_w = (ssd_norm_w[0].reshape(D_INNER, 1) * ssd_out_w[0]).astype(BF16)
    x2 = ssd_out_mm(y_l.reshape(batch * seq, D_INNER), proj_l, out_w, x2, lat[2], seq, 1024, "ssd_out_lat")
    c2 = ssd_out_mm(y_c.reshape(batch * ctx_len, D_INNER), proj_c, out_w, c2, cx[2], batch * ctx_len, 1024,
                    "ssd_out_ctx")

    rwt = moe_router_w[0].T
    w1, w3, w2 = moe_w1.astype(BF16), moe_w3.astype(BF16), moe_w2.astype(BF16)
    x3 = moe_block(x2.reshape(batch, seq, d), n2 * (1.0 + lat[4]), lat[3], lat[5], rwt, 0, w1, w3, w2,
                   final_w, 1, False, "moe0_lat")
    c3 = moe_block(c2.reshape(batch, ctx_len, d), n2 * (1.0 + cx[4]), cx[3], cx[5], rwt, 0, w1, w3, w2,
                   final_w, math.gcd(batch, 8), False, "moe0_ctx")
    x2 = x3.reshape(batch * seq, d)
    c2 = c3.reshape(batch * ctx_len, d)

    lat, cx = mod_parts(1)
    n1 = norm1_w[1].reshape(1, 1, d)
    n2 = norm2_w[1].reshape(1, 1, d)
    qkv_w = attn_qkv_w[0].astype(BF16)
    qkv_l, = norm_mod_mm(x2, n1 * (1.0 + lat[1]), lat[0], qkv_w, seq, 2048, 768, "qkv_lat")
    qkv_c, = norm_mod_mm(c2, n1 * (1.0 + cx[1]), cx[0], qkv_w, batch * ctx_len, 1024, 768, "qkv_ctx")
    tables = _attn_tables(seq)
    q_l, k_l, v_l = qk_prep(qkv_l, attn_q_norm_w[0], attn_k_norm_w[0], tables, seq, True, 512, "qk_prep_lat")
    _, k_c, v_c = qk_prep(qkv_c, attn_q_norm_w[0], attn_k_norm_w[0], tables, ctx_len, False, ctx_len,
                          "qk_prep_ctx")
    o = attention(q_l, k_l, v_l, k_c, v_c, batch, seq, ctx_len, 256)
    x2 = res_mm(o, attn_out_w[0].astype(BF16), x2, lat[2], seq, 1024, "attn_out")

    rwt = moe_router_w[1].T
    x3 = moe_block(x2.reshape(batch, seq, d), n2 * (1.0 + lat[4]), lat[3], lat[5], rwt, 1, w1, w3, w2,
                   final_w, 1, True, "moe1_lat")
    return x3
```
